```python
import jax, jax.numpy as jnp
from jax import lax
import numpy as np

D_MODEL = 4096
BATCH = 2
SEQ = 8192
DEPTH = 2

N_A_LAYERS = DEPTH // 2
N_B_LAYERS = DEPTH - N_A_LAYERS
EPS = 1e-6

GLA_HEADS = 8
GLA_DK = D_MODEL // 2
GLA_DV = D_MODEL
GLA_HEAD_K = GLA_DK // GLA_HEADS
GLA_HEAD_V = GLA_DV // GLA_HEADS
GLA_GATE_RANK = 16
GLA_GATE_TAU = 16.0
GLA_CHUNK = 64
GLA_IN = 2 * GLA_DK + 2 * GLA_DV + GLA_GATE_RANK

SWA_HEAD_DIM = 64
SWA_Q_HEADS = D_MODEL // SWA_HEAD_DIM
SWA_KV_HEADS = 8
SWA_GROUP = SWA_Q_HEADS // SWA_KV_HEADS
SWA_WINDOW = 128
SWA_BLOCK = 128

MOE_GROUPS = 4
MOE_EXPERTS_PER_GROUP = 8
MOE_EXPERTS = MOE_GROUPS * MOE_EXPERTS_PER_GROUP
MOE_TOP_K = 2
MOE_D_EXPERT = D_MODEL // 8

kernel_name = "yoco_gla_swa_sink_hier_moe"


def rms_norm(x, g):
    xf = x.astype(jnp.float32)
    y = xf * lax.rsqrt(jnp.mean(xf * xf, axis=-1, keepdims=True) + EPS)
    return (y * g.astype(jnp.float32)).astype(x.dtype)


def _to_chunks(a):
    b, t, h, d = a.shape
    return a.reshape(b, t // GLA_CHUNK, GLA_CHUNK, h, d).transpose(1, 0, 3, 2, 4)


def chunked_gla(q, k, v, log_a):
    bsz, t, h, dk = q.shape
    dv = v.shape[-1]
    causal = jnp.tril(jnp.ones((GLA_CHUNK, GLA_CHUNK), dtype=bool))[:, :, None]

    def step(state, inp):
        qc, kc, vc, gc = inp
        cum = jnp.cumsum(gc, axis=2)
        inter = jnp.einsum('bhtd,bhde->bhte', qc * jnp.exp(cum), state)
        rel = jnp.where(causal, cum[:, :, :, None, :] - cum[:, :, None, :, :], -jnp.inf)
        scores = jnp.einsum('bhtd,bhsd,bhtsd->bhts', qc, kc, jnp.exp(rel))
        out = inter + jnp.einsum('bhts,bhse->bhte', scores, vc)
        last = cum[:, :, -1:, :]
        new_state = state * jnp.exp(last[:, :, 0, :])[..., None] + jnp.einsum(
            'bhsd,bhse->bhde', kc * jnp.exp(last - cum), vc)
        return new_state, out

    s0 = jnp.zeros((bsz, h, dk, dv), jnp.float32)
    _, o = lax.scan(step, s0, (_to_chunks(q), _to_chunks(k), _to_chunks(v), _to_chunks(log_a)))
    return o.transpose(1, 0, 3, 2, 4).reshape(bsz, t, h, dv)


def gla_mixer(xn, w_in, w_decay_up, b_decay, g_out, w_out):
    bsz, t, _ = xn.shape
    proj = xn @ w_in
    q, k, v, r, z = jnp.split(
        proj, [GLA_DK, 2 * GLA_DK, 2 * GLA_DK + GLA_DV, 2 * GLA_DK + 2 * GLA_DV], axis=-1)
    log_a = jax.nn.log_sigmoid((z @ w_decay_up + b_decay).astype(jnp.float32)) / GLA_GATE_TAU

    def heads(a, d):
        return a.astype(jnp.float32).reshape(bsz, t, GLA_HEADS, d)

    o = chunked_gla(heads(q, GLA_HEAD_K) * GLA_HEAD_K ** -0.5, heads(k, GLA_HEAD_K),
                    heads(v, GLA_HEAD_V), heads(log_a, GLA_HEAD_K))
    o = rms_norm(o, g_out) * jax.nn.silu(heads(r, GLA_HEAD_V))
    return o.reshape(bsz, t, GLA_DV).astype(xn.dtype) @ w_out


def shared_kv(h, g_kv, w_kv, g_k):
    bsz, t, _ = h.shape
    kv = rms_norm(h, g_kv) @ w_kv
    k, v = jnp.split(kv, 2, axis=-1)
    k = rms_norm(k.reshape(bsz, t, SWA_KV_HEADS, SWA_HEAD_DIM), g_k)
    return k, v.reshape(bsz, t, SWA_KV_HEADS, SWA_HEAD_DIM)


def _band(a):
    bsz, t = a.shape[:2]
    blocks = a.reshape(bsz, t // SWA_BLOCK, SWA_BLOCK, SWA_KV_HEADS, SWA_HEAD_DIM)
    prev = jnp.pad(blocks, ((0, 0), (1, 0), (0, 0), (0, 0), (0, 0)))[:, :-1]
    return jnp.concatenate([prev, blocks], axis=2).transpose(1, 0, 2, 3, 4)


def sliding_window_sinks(q, k, v, sinks):
    bsz, t = q.shape[:2]
    nb = t // SWA_BLOCK
    qb = q.reshape(bsz, nb, SWA_BLOCK, SWA_KV_HEADS, SWA_GROUP, SWA_HEAD_DIM).transpose(1, 0, 2, 3, 4, 5)
    qi = jnp.arange(SWA_BLOCK)[:, None]
    kr = jnp.arange(2 * SWA_BLOCK)[None, :]
    dist = SWA_BLOCK + qi - kr
    band = (dist >= 0) & (dist < SWA_WINDOW)
    sink = sinks.astype(jnp.float32).reshape(SWA_KV_HEADS, SWA_GROUP)[None, :, :, None, None]

    def block(inp):
        j, qj, kj, vj = inp
        s = jnp.einsum('bqhgd,bkhd->bhgqk', qj, kj, preferred_element_type=jnp.float32)
        valid = band & ((kr >= SWA_BLOCK) | (j > 0))
        s = jnp.where(valid, s, -jnp.inf)
        m = jnp.maximum(jnp.max(s, axis=-1, keepdims=True), sink)
        p = jnp.exp(s - m)
        p = p / (jnp.sum(p, axis=-1, keepdims=True) + jnp.exp(sink - m))
        return jnp.einsum('bhgqk,bkhd->bqhgd', p.astype(vj.dtype), vj)

    o = lax.map(block, (jnp.arange(nb), qb, _band(k), _band(v)))
    return o.transpose(1, 0, 2, 3, 4, 5).reshape(bsz, t, SWA_Q_HEADS * SWA_HEAD_DIM)


def swa_mixer(xn, k_sh, v_sh, w_q, g_q, sinks, w_out):
    bsz, t, _ = xn.shape
    q = (xn @ w_q).reshape(bsz, t, SWA_KV_HEADS, SWA_GROUP, SWA_HEAD_DIM)
    q = rms_norm(q, g_q) * SWA_HEAD_DIM ** -0.5
    return sliding_window_sinks(q, k_sh, v_sh, sinks) @ w_out


def hier_moe(xn, w_group, b_group, w_expert, b_expert, w_gate_up, w_down):
    bsz, t, d = xn.shape
    xf = xn.reshape(-1, d)
    n = xf.shape[0]
    p_group = jax.nn.softmax((xf @ w_group).astype(jnp.float32) + b_group.astype(jnp.float32), axis=-1)
    p_top, g_idx = lax.top_k(p_group, 1)
    e_logits = ((xf @ w_expert).astype(jnp.float32) + b_expert.astype(jnp.float32)).reshape(
        n, MOE_GROUPS, MOE_EXPERTS_PER_GROUP)
    e_in = jnp.take_along_axis(e_logits, g_idx[:, :, None], axis=1)[:, 0]
    top_v, top_i = lax.top_k(e_in, MOE_TOP_K)
    weights = jax.nn.softmax(top_v, axis=-1) * p_top
    ids = g_idx * MOE_EXPERTS_PER_GROUP + top_i
    gate = jnp.sum(jax.nn.one_hot(ids, MOE_EXPERTS, dtype=jnp.float32) * weights[..., None], axis=1)

    def expert(acc, inp):
        w1, w2, ge = inp
        a, u = jnp.split(xf @ w1, 2, axis=-1)
        return acc + ge[:, None] * ((jax.nn.silu(a) * u) @ w2), None

    y, _ = lax.scan(expert, jnp.zeros_like(xf), (w_gate_up, w_down, gate.T.astype(xf.dtype)))
    return y.reshape(bsz, t, d)


def setup_inputs(seed: int = 0) -> dict:
    key = jax.random.key(seed)
    ks = iter(jax.random.split(key, 32))

    def nrm(shape, scale):
        return jax.random.normal(next(ks), shape, jnp.float32) * scale

    def gain(shape):
        return 1.0 + nrm(shape, 0.02)

    d = D_MODEL
    attn_w = SWA_Q_HEADS * SWA_HEAD_DIM
    return {
        "x": nrm((BATCH, SEQ, d), 1.0),
        "a_norm_g": gain((N_A_LAYERS, d)),
        "a_w_in": nrm((N_A_LAYERS, d, GLA_IN), d ** -0.5),
        "a_w_decay_up": nrm((N_A_LAYERS, GLA_GATE_RANK, GLA_DK), GLA_GATE_RANK ** -0.5),
        "a_b_decay": nrm((N_A_LAYERS, GLA_DK), 0.1),
        "a_out_norm_g": gain((N_A_LAYERS, GLA_HEAD_V)),
        "a_w_out": nrm((N_A_LAYERS, GLA_DV, d), GLA_DV ** -0.5),
        "kv_norm_g": gain((d,)),
        "w_kv": nrm((d, 2 * SWA_KV_HEADS * SWA_HEAD_DIM), d ** -0.5),
        "k_norm_g": gain((SWA_HEAD_DIM,)),
        "b_norm_g": gain((N_B_LAYERS, d)),
        "b_w_q": nrm((N_B_LAYERS, d, attn_w), d ** -0.5),
        "b_q_norm_g": gain((N_B_LAYERS, SWA_HEAD_DIM)),
        "b_sinks": nrm((N_B_LAYERS, SWA_Q_HEADS), 0.5),
        "b_w_out": nrm((N_B_LAYERS, attn_w, d), attn_w ** -0.5),
        "ffn_norm_g": gain((DEPTH, d)),
        "moe_w_group": nrm((DEPTH, d, MOE_GROUPS), d ** -0.5),
        "moe_b_group": nrm((DEPTH, MOE_GROUPS), 0.01),
        "moe_w_expert": nrm((DEPTH, d, MOE_EXPERTS), d ** -0.5),
        "moe_b_expert": nrm((DEPTH, MOE_EXPERTS), 0.01),
        "moe_w_gate_up": nrm((DEPTH, MOE_EXPERTS, d, 2 * MOE_D_EXPERT), d ** -0.5),
        "moe_w_down": nrm((DEPTH, MOE_EXPERTS, MOE_D_EXPERT, d), MOE_D_EXPERT ** -0.5),
    }


def reference(x, a_norm_g, a_w_in, a_w_decay_up, a_b_decay, a_out_norm_g, a_w_out,
              kv_norm_g, w_kv, k_norm_g, b_norm_g, b_w_q, b_q_norm_g, b_sinks, b_w_out,
              ffn_norm_g, moe_w_group, moe_b_group, moe_w_expert, moe_b_expert,
              moe_w_gate_up, moe_w_down):
    h = x
    k_sh = None
    v_sh = None
    for layer in range(DEPTH):
        if layer < N_A_LAYERS:
            h = h + gla_mixer(rms_norm(h, a_norm_g[layer]), a_w_in[layer], a_w_decay_up[layer],
                              a_b_decay[layer], a_out_norm_g[layer], a_w_out[layer])
        else:
            lb = layer - N_A_LAYERS
            if lb == 0:
                k_sh, v_sh = shared_kv(h, kv_norm_g, w_kv, k_norm_g)
            h = h + swa_mixer(rms_norm(h, b_norm_g[lb]), k_sh, v_sh, b_w_q[lb], b_q_norm_g[lb],
                              b_sinks[lb], b_w_out[lb])
        h = h + hier_moe(rms_norm(h, ffn_norm_g[layer]), moe_w_group[layer], moe_b_group[layer],
                         moe_w_expert[layer], moe_b_expert[layer], moe_w_gate_up[layer],
                         moe_w_down[layer])
    return h
```

```python
import functools
import math

import numpy as np
import jax
import jax.numpy as jnp
from jax import lax
from jax.experimental import pallas as pl
from jax.experimental.pallas import tpu as pltpu

F32 = jnp.float32
BF16 = jnp.bfloat16
EPS = 1e-6

GLA_HEADS = 8
GLA_GATE_RANK = 16
GLA_GATE_TAU = 16.0
SWA_HEAD_DIM = 64
SWA_KV_HEADS = 8
SWA_WINDOW = 128
MOE_GROUPS = 4
MOE_EXPERTS_PER_GROUP = 8
MOE_EXPERTS = MOE_GROUPS * MOE_EXPERTS_PER_GROUP

LANES = 128
VMEM_LIMIT_BYTES = 56 * 1024 * 1024

GLA_CHUNK = 64
NEG_BIG = -1e30


def _cparams(sem):
    return pltpu.CompilerParams(dimension_semantics=sem, vmem_limit_bytes=VMEM_LIMIT_BYTES)


def _split3(x):
    hi = x.astype(BF16)
    r1 = x - hi.astype(F32)
    mid = r1.astype(BF16)
    lo = (r1 - mid.astype(F32)).astype(BF16)
    return hi, mid, lo


def _dot_nt(a, b):
    return lax.dot_general(a, b, (((1,), (1,)), ((), ())), preferred_element_type=F32)


def _rmsnorm_body(x_ref, g_ref, *o_refs):
    x = x_ref[...]
    y = x * lax.rsqrt(jnp.mean(x * x, axis=-1, keepdims=True) + EPS)
    for i, o_ref in enumerate(o_refs):
        o_ref[...] = (y * g_ref[i:i + 1, :]).astype(o_ref.dtype)


def rmsnorm(x, gains, out_dtype=BF16, tm=256):
    n, d = x.shape
    k = gains.shape[0]
    outs = pl.pallas_call(
        _rmsnorm_body,
        grid=(n // tm,),
        in_specs=[pl.BlockSpec((tm, d), lambda i: (i, 0)), pl.BlockSpec((k, d), lambda i: (0, 0))],
        out_specs=[pl.BlockSpec((tm, d), lambda i: (i, 0)) for _ in range(k)],
        out_shape=[jax.ShapeDtypeStruct((n, d), out_dtype) for _ in range(k)],
        compiler_params=_cparams(("parallel",)),
        name="rmsnorm",
    )(x, gains.astype(F32))
    return outs


def _mm_body(*refs, has_res, head_dim, scale):
    a_ref, w_ref = refs[0], refs[1]
    pos = 2
    res_ref = None
    if has_res:
        res_ref = refs[pos]
        pos += 1
    if head_dim:
        gain_ref, ind_ref, indt_ref = refs[pos:pos + 3]
        pos += 3
    o_ref = refs[pos]
    y = jnp.dot(a_ref[...], w_ref[...], preferred_element_type=F32)
    if head_dim:
        ssq = jnp.dot((y * y).astype(BF16), ind_ref[...], preferred_element_type=F32)
        inv = lax.rsqrt(ssq * (1.0 / head_dim) + EPS)
        h, m, l = _split3(inv)
        it = indt_ref[...]
        inv_full = (jnp.dot(h, it, preferred_element_type=F32) + jnp.dot(m, it, preferred_element_type=F32)
                    + jnp.dot(l, it, preferred_element_type=F32))
        y = y * inv_full * (gain_ref[...] * scale)
    if has_res:
        y = y + res_ref[...]
    o_ref[...] = y.astype(o_ref.dtype)


def matmul(a, w, *, n_cols, col_block_off=0, tm, tn, res=None, out_dtype=F32,
           head_dim=0, head_gain=None, scale=1.0):
    n, k = a.shape
    assert n % tm == 0 and n_cols % tn == 0
    in_specs = [pl.BlockSpec((tm, k), lambda i, j: (i, 0)),
                pl.BlockSpec((k, tn), lambda i, j: (0, j + col_block_off))]
    args = [a, w]
    if res is not None:
        in_specs.append(pl.BlockSpec((tm, tn), lambda i, j: (i, j)))
        args.append(res)
    if head_dim:
        assert tn % head_dim == 0 and tn // head_dim <= LANES
        nh = tn // head_dim
        ind = np.zeros((tn, LANES), np.float32)
        ind[np.arange(tn), np.arange(tn) // head_dim] = 1.0
        gain_row = jnp.tile(head_gain.astype(F32), nh).reshape(1, tn)
        in_specs += [pl.BlockSpec((1, tn), lambda i, j: (0, 0)),
                     pl.BlockSpec((tn, LANES), lambda i, j: (0, 0)),
                     pl.BlockSpec((LANES, tn), lambda i, j: (0, 0))]
        args += [gain_row, jnp.asarray(ind, BF16), jnp.asarray(ind.T, BF16)]
    return pl.pallas_call(
        functools.partial(_mm_body, has_res=res is not None, head_dim=head_dim, scale=scale),
        grid=(n // tm, n_cols // tn),
        in_specs=in_specs,
        out_specs=pl.BlockSpec((tm, tn), lambda i, j: (i, j)),
        out_shape=jax.ShapeDtypeStruct((n, n_cols), out_dtype),
        compiler_params=_cparams(("parallel", "arbitrary")),
        name="matmul",
    )(*args)


def _gla_tables(c):
    levels = []
    l = c // 2
    while l >= 1:
        levels.append(l)
        l //= 2
    rows = np.arange(c)
    w = [(rows[None, :] <= rows[:, None]).astype(np.float32)]
    masks = [np.eye(c, dtype=np.float32)]
    for l in levels:
        blk = rows // l
        odd = (blk % 2) == 1
        ref = np.where(odd, blk * l, (blk + 1) * l)
        lo = np.minimum(rows, ref)
        hi = np.maximum(rows, ref)
        w.append(((rows[None, :] > lo[:, None]) & (rows[None, :] <= hi[:, None])).astype(np.float32))
        masks.append((odd[:, None] & (blk[None, :] == blk[:, None] - 1)).astype(np.float32))
    return levels, np.concatenate(w, axis=0), np.stack(masks, axis=0)


def _gla_body(q_ref, k_ref, v_ref, r_ref, z_ref, wup_ref, bd_ref, gout_ref, wall_ref, mask_ref,
              o_ref, st_ref, *, c, levels, scale, tau):
    tb, dk = q_ref.shape
    dv = v_ref.shape[1]

    @pl.when(pl.program_id(2) == 0)
    def _():
        st_ref[...] = jnp.zeros_like(st_ref)

    row = lax.broadcasted_iota(jnp.int32, (c, dk), 0)

    def chunk(ci, carry):
        r0 = pl.multiple_of(ci * c, c)
        q = q_ref[pl.ds(r0, c), :].astype(F32) * scale
        k = k_ref[pl.ds(r0, c), :].astype(F32)
        v = v_ref[pl.ds(r0, c), :]
        zc = z_ref[pl.ds(r0, c), :]
        zh, zm, zl = _split3(zc)
        wh, wm, wl = wup_ref[0], wup_ref[1], wup_ref[2]
        pre = (jnp.dot(zh, wh, preferred_element_type=F32) + jnp.dot(zh, wm, preferred_element_type=F32)
               + jnp.dot(zm, wh, preferred_element_type=F32) + jnp.dot(zh, wl, preferred_element_type=F32)
               + jnp.dot(zm, wm, preferred_element_type=F32) + jnp.dot(zl, wh, preferred_element_type=F32))
        pre = pre + bd_ref[...]
        g = (jnp.minimum(pre, 0.0) - jnp.log(1.0 + jnp.exp(-jnp.abs(pre)))) * (1.0 / tau)
        g_hi = g.astype(BF16)
        g_lo = (g - g_hi.astype(F32)).astype(BF16)
        e_all = jnp.dot(wall_ref[...], jnp.concatenate([g_hi, g_lo], axis=1), preferred_element_type=F32)
        e_all = e_all[:, :dk] + e_all[:, dk:]
        ecum = e_all[0:c]
        last = ecum[c - 1:c, :]
        st = st_ref[...]
        qe = (q * jnp.exp(ecum)).astype(BF16)
        inter = _dot_nt(qe, st.astype(BF16))
        a = mask_ref[0] * _dot_nt(q.astype(BF16), k.astype(BF16))
        for li, l in enumerate(levels):
            el = e_all[(li + 1) * c:(li + 2) * c]
            odd = ((row >> int(math.log2(l))) & 1) == 1
            x = (jnp.where(odd, q, k) * jnp.exp(el)).astype(BF16)
            a = a + mask_ref[li + 1] * _dot_nt(x, x)
        o = inter + jnp.dot(a.astype(BF16), v, preferred_element_type=F32)
        o = o * lax.rsqrt(jnp.mean(o * o, axis=-1, keepdims=True) + EPS) * gout_ref[...]
        r = r_ref[pl.ds(r0, c), :].astype(F32)
        o_ref[pl.ds(r0, c), :] = (o * (r / (1.0 + jnp.exp(-r)))).astype(o_ref.dtype)
        krev = (k * jnp.exp(last - ecum)).astype(BF16)
        upd = lax.dot_general(v, krev, (((0,), (0,)), ((), ())), preferred_element_type=F32)
        st_ref[...] = st * jnp.exp(last) + upd
        return carry

    lax.fori_loop(0, tb // c, chunk, 0)


def gla_core(proj, z, w_up, b_decay, g_out, *, batch, heads, dk, dv, tb=512, c=GLA_CHUNK):
    n = proj.shape[0]
    t = n // batch
    nb = t // tb
    levels, wall, masks = _gla_tables(c)
    rank = w_up.shape[0]
    wup_pad = jnp.zeros((LANES, heads * dk), F32).at[:rank].set(w_up.astype(F32))
    wup3 = jnp.stack(_split3(wup_pad), axis=0)
    kq = heads * dk // dk
    kv0 = 2 * heads * dk // dv
    row_map = lambda b, h, i: b * nb + i
    return pl.pallas_call(
        functools.partial(_gla_body, c=c, levels=levels, scale=dk ** -0.5, tau=GLA_GATE_TAU),
        grid=(batch, heads, nb),
        in_specs=[
            pl.BlockSpec((tb, dk), lambda b, h, i: (row_map(b, h, i), h)),
            pl.BlockSpec((tb, dk), lambda b, h, i: (row_map(b, h, i), kq + h)),
            pl.BlockSpec((tb, dv), lambda b, h, i: (row_map(b, h, i), kv0 + h)),
            pl.BlockSpec((tb, dv), lambda b, h, i: (row_map(b, h, i), kv0 + heads + h)),
            pl.BlockSpec((tb, LANES), lambda b, h, i: (row_map(b, h, i), 0)),
            pl.BlockSpec((3, LANES, dk), lambda b, h, i: (0, 0, h)),
            pl.BlockSpec((1, dk), lambda b, h, i: (0, h)),
            pl.BlockSpec((1, dv), lambda b, h, i: (0, 0)),
            pl.BlockSpec(wall.shape, lambda b, h, i: (0, 0)),
            pl.BlockSpec(masks.shape, lambda b, h, i: (0, 0, 0)),
        ],
        out_specs=pl.BlockSpec((tb, dv), lambda b, h, i: (row_map(b, h, i), h)),
        out_shape=jax.ShapeDtypeStruct((n, heads * dv), BF16),
        scratch_shapes=[pltpu.VMEM((dv, dk), F32)],
        compiler_params=_cparams(("parallel", "parallel", "arbitrary")),
        name="gla_core",
    )(proj, proj, proj, proj, z, wup3, b_decay.reshape(1, -1).astype(F32), g_out.reshape(1, -1).astype(F32),
      jnp.asarray(wall, BF16), jnp.asarray(masks, F32))


def _swa_body(sink_ref, q_ref, kc_ref, kp_ref, vc_ref, vp_ref, bias_ref, o_ref, *, group, blk):
    h = pl.program_id(1)
    hd = q_ref.shape[-1]
    q = q_ref[...].reshape(group * blk, hd)
    k = jnp.concatenate([kp_ref[...], kc_ref[...]], axis=0)
    v = jnp.concatenate([vp_ref[...], vc_ref[...]], axis=0)
    s = _dot_nt(q, k) + bias_ref[...]
    ps = []
    for g in range(group):
        sg = s[g * blk:(g + 1) * blk]
        sink = sink_ref[h * group + g]
        m = jnp.maximum(jnp.max(sg, axis=-1, keepdims=True), sink)
        p = jnp.exp(sg - m)
        denom = jnp.sum(p, axis=-1, keepdims=True) + jnp.exp(sink - m)
        ps.append((p / denom).astype(BF16))
    p = jnp.concatenate(ps, axis=0)
    o = jnp.dot(p, v, preferred_element_type=F32)
    o_ref[...] = o.reshape(group, blk, hd).astype(o_ref.dtype)


def swa_attention(q, k, v, sinks, *, blk=SWA_WINDOW):
    b, hq, t, hd = q.shape
    kvh = k.shape[1]
    group = hq // kvh
    nb = t // blk
    qi = np.arange(blk)[:, None]
    kr = np.arange(2 * blk)[None, :]
    dist = blk + qi - kr
    band = (dist >= 0) & (dist < blk)
    bias = np.stack([np.where(band & (kr >= blk), 0.0, NEG_BIG), np.where(band, 0.0, NEG_BIG)], axis=0)
    bias = np.tile(bias.astype(np.float32), (1, group, 1))
    prev = lambda bi, h, j: (bi, h, jnp.maximum(j - 1, 0), 0)
    cur = lambda bi, h, j: (bi, h, j, 0)
    kv_spec_c = pl.BlockSpec((None, None, blk, hd), cur)
    kv_spec_p = pl.BlockSpec((None, None, blk, hd), prev)
    return pl.pallas_call(
        functools.partial(_swa_body, group=group, blk=blk),
        grid_spec=pltpu.PrefetchScalarGridSpec(
            num_scalar_prefetch=0,
            grid=(b, kvh, nb),
            in_specs=[
                pl.BlockSpec(memory_space=pltpu.SMEM),
                pl.BlockSpec((None, group, blk, hd), cur),
                kv_spec_c, kv_spec_p, kv_spec_c, kv_spec_p,
                pl.BlockSpec((None, group * blk, 2 * blk), lambda bi, h, j: (jnp.minimum(j, 1), 0, 0)),
            ],
            out_specs=pl.BlockSpec((None, group, blk, hd), cur),
        ),
        out_shape=jax.ShapeDtypeStruct((b, hq, t, hd), BF16),
        compiler_params=_cparams(("parallel", "parallel", "arbitrary")),
        name="swa_attention",
    )(sinks.astype(F32), q, k, k, v, v, jnp.asarray(bias))


def _router_body(x_ref, g_ref, wr_ref, br_ref, xn_ref, ids_ref, wts_ref, *, groups, per_group):
    x = x_ref[...]
    xn = x * lax.rsqrt(jnp.mean(x * x, axis=-1, keepdims=True) + EPS) * g_ref[...]
    xn_ref[...] = xn
    xh, xm, xl = _split3(xn)
    wh, wm, wl = wr_ref[0], wr_ref[1], wr_ref[2]
    logits = (jnp.dot(xh, wh, preferred_element_type=F32) + jnp.dot(xh, wm, preferred_element_type=F32)
              + jnp.dot(xm, wh, preferred_element_type=F32) + jnp.dot(xh, wl, preferred_element_type=F32)
              + jnp.dot(xm, wm, preferred_element_type=F32) + jnp.dot(xl, wh, preferred_element_type=F32))
    logits = logits + br_ref[...]
    lane = lax.broadcasted_iota(jnp.int32, logits.shape, 1).astype(F32)
    big = jnp.float32(LANES)
    neg = jnp.float32(-jnp.inf)
    gl = jnp.where(lane < groups, logits, neg)
    gm = jnp.max(gl, axis=-1, keepdims=True)
    p_top = 1.0 / jnp.sum(jnp.exp(gl - gm), axis=-1, keepdims=True)
    g_idx = jnp.min(jnp.where(gl == gm, lane, big), axis=-1, keepdims=True)
    lo = groups + g_idx * per_group
    el = jnp.where((lane >= lo) & (lane < lo + per_group), logits, neg)
    v1 = jnp.max(el, axis=-1, keepdims=True)
    i1 = jnp.min(jnp.where(el == v1, lane, big), axis=-1, keepdims=True)
    el2 = jnp.where(lane == i1, neg, el)
    v2 = jnp.max(el2, axis=-1, keepdims=True)
    i2 = jnp.min(jnp.where(el2 == v2, lane, big), axis=-1, keepdims=True)
    e21 = jnp.exp(v2 - v1)
    w1 = p_top / (1.0 + e21)
    w2 = p_top * e21 / (1.0 + e21)
    ids_ref[...] = jnp.where(lane == 0, i1 - groups, jnp.where(lane == 1, i2 - groups, 0.0)).astype(jnp.int32)
    wts_ref[...] = jnp.where(lane == 0, w1, jnp.where(lane == 1, w2, 0.0))


def router(h, gain, w_group, b_group, w_expert, b_expert, tm=256):
    n, d = h.shape
    groups = w_group.shape[1]
    experts = w_expert.shape[1]
    wr = jnp.zeros((d, LANES), F32).at[:, :groups].set(w_group).at[:, groups:groups + experts].set(w_expert)
    br = jnp.zeros((1, LANES), F32).at[0, :groups].set(b_group).at[0, groups:groups + experts].set(b_expert)
    wr3 = jnp.stack(_split3(wr), axis=0)
    row = lambda i: (i, 0)
    return pl.pallas_call(
        functools.partial(_router_body, groups=groups, per_group=experts // groups),
        grid=(n // tm,),
        in_specs=[pl.BlockSpec((tm, d), row), pl.BlockSpec((1, d), lambda i: (0, 0)),
                  pl.BlockSpec((3, d, LANES), lambda i: (0, 0, 0)), pl.BlockSpec((1, LANES), lambda i: (0, 0))],
        out_specs=[pl.BlockSpec((tm, d), row), pl.BlockSpec((tm, LANES), row), pl.BlockSpec((tm, LANES), row)],
        out_shape=[jax.ShapeDtypeStruct((n, d), F32), jax.ShapeDtypeStruct((n, LANES), jnp.int32),
                   jax.ShapeDtypeStruct((n, LANES), F32)],
        compiler_params=_cparams(("parallel",)),
        name="moe_router",
    )(h, gain.reshape(1, d).astype(F32), wr3, br)


def _row_copy(src_hbm, src_row, dst_ref, dst_row, sem):
    return pltpu.make_async_copy(src_hbm.at[pl.ds(src_row, 1), :], dst_ref.at[pl.ds(dst_row, 1), :], sem)


def _gather_body(idx_ref, x_hbm, o_ref, sem, *, tm):
    def start(r, c):
        _row_copy(x_hbm, idx_ref[0, 0, r], o_ref, r, sem).start()
        return c
    lax.fori_loop(0, tm, start, 0)

    def wait(r, c):
        _row_copy(x_hbm, 0, o_ref, r, sem).wait()
        return c
    lax.fori_loop(0, tm, wait, 0)


def gather_rows(x, idx, tm):
    p = idx.shape[0]
    d = x.shape[1]
    nt = p // tm
    return pl.pallas_call(
        functools.partial(_gather_body, tm=tm),
        grid=(nt,),
        in_specs=[pl.BlockSpec((1, 1, tm), lambda t: (t, 0, 0), memory_space=pltpu.SMEM),
                  pl.BlockSpec(memory_space=pl.ANY)],
        out_specs=pl.BlockSpec((tm, d), lambda t: (t, 0)),
        out_shape=jax.ShapeDtypeStruct((p, d), x.dtype),
        scratch_shapes=[pltpu.SemaphoreType.DMA(())],
        compiler_params=_cparams(("arbitrary",)),
        name="moe_gather",
    )(idx.reshape(nt, 1, tm), x)


def _moe_up_body(te_ref, x_ref, wg_ref, wu_ref, o_ref, wgb_ref, wub_ref):
    t = pl.program_id(1)
    changed = te_ref[t] != te_ref[jnp.maximum(t - 1, 0)]

    @pl.when((t == 0) | changed)
    def _():
        wgb_ref[...] = wg_ref[...].astype(BF16)
        wub_ref[...] = wu_ref[...].astype(BF16)

    x = x_ref[...].astype(BF16)
    a = jnp.dot(x, wgb_ref[...], preferred_element_type=F32)
    u = jnp.dot(x, wub_ref[...], preferred_element_type=F32)
    o_ref[...] = (a / (1.0 + jnp.exp(-a)) * u).astype(o_ref.dtype)


def moe_up(xs, tile_expert, w_gate_up, tm, fc):
    p, d = xs.shape
    f = w_gate_up.shape[2] // 2
    nc = f // fc
    return pl.pallas_call(
        _moe_up_body,
        grid_spec=pltpu.PrefetchScalarGridSpec(
            num_scalar_prefetch=1,
            grid=(nc, p // tm),
            in_specs=[pl.BlockSpec((tm, d), lambda c, t, te: (t, 0)),
                      pl.BlockSpec((None, d, fc), lambda c, t, te: (te[t], 0, c)),
                      pl.BlockSpec((None, d, fc), lambda c, t, te: (te[t], 0, nc + c))],
            out_specs=pl.BlockSpec((tm, fc), lambda c, t, te: (t, c)),
            scratch_shapes=[pltpu.VMEM((d, fc), BF16), pltpu.VMEM((d, fc), BF16)],
        ),
        out_shape=jax.ShapeDtypeStruct((p, f), BF16),
        compiler_params=_cparams(("arbitrary", "arbitrary")),
        name="moe_up",
    )(tile_expert, xs, w_gate_up, w_gate_up)


def _moe_down_body(te_ref, h_ref, w_ref, o_ref, wb_ref):
    t = pl.program_id(0)
    changed = te_ref[t] != te_ref[jnp.maximum(t - 1, 0)]

    @pl.when((t == 0) | changed)
    def _():
        wb_ref[...] = w_ref[...].astype(BF16)

    o_ref[...] = jnp.dot(h_ref[...], wb_ref[...], preferred_element_type=F32).astype(o_ref.dtype)


def moe_down(hs, tile_expert, w_down, tm):
    p, f = hs.shape
    d = w_down.shape[2]
    return pl.pallas_call(
        _moe_down_body,
        grid_spec=pltpu.PrefetchScalarGridSpec(
            num_scalar_prefetch=1,
            grid=(p // tm,),
            in_specs=[pl.BlockSpec((tm, f), lambda t, te: (t, 0)),
                      pl.BlockSpec((None, f, d), lambda t, te: (te[t], 0, 0))],
            out_specs=pl.BlockSpec((tm, d), lambda t, te: (t, 0)),
            scratch_shapes=[pltpu.VMEM((f, d), BF16)],
        ),
        out_shape=jax.ShapeDtypeStruct((p, d), F32),
        compiler_params=_cparams(("arbitrary",)),
        name="moe_down",
    )(tile_expert, hs, w_down)


def _combine_body(pos_ref, h_ref, w_ref, y_hbm, o_ref, buf_ref, sem, *, tm):
    def start(r, c):
        _row_copy(y_hbm, pos_ref[0, 0, r], buf_ref.at[0], r, sem).start()
        _row_copy(y_hbm, pos_ref[0, 0, tm + r], buf_ref.at[1], r, sem).start()
        return c
    lax.fori_loop(0, tm, start, 0)

    def wait(r, c):
        _row_copy(y_hbm, 0, buf_ref.at[0], r, sem).wait()
        _row_copy(y_hbm, 0, buf_ref.at[1], r, sem).wait()
        return c
    lax.fori_loop(0, tm, wait, 0)
    w = w_ref[...]
    o_ref[...] = h_ref[...] + w[:, 0:1] * buf_ref[0] + w[:, 1:2] * buf_ref[1]


def moe_combine(h, wts, ys, pos, tm=256):
    n, d = h.shape
    nt = n // tm
    pos_t = pos.reshape(nt, tm, 2).transpose(0, 2, 1).reshape(nt, 1, 2 * tm)
    return pl.pallas_call(
        functools.partial(_combine_body, tm=tm),
        grid=(nt,),
        in_specs=[pl.BlockSpec((1, 1, 2 * tm), lambda t: (t, 0, 0), memory_space=pltpu.SMEM),
                  pl.BlockSpec((tm, d), lambda t: (t, 0)),
                  pl.BlockSpec((tm, LANES), lambda t: (t, 0)),
                  pl.BlockSpec(memory_space=pl.ANY)],
        out_specs=pl.BlockSpec((tm, d), lambda t: (t, 0)),
        out_shape=jax.ShapeDtypeStruct((n, d), F32),
        scratch_shapes=[pltpu.VMEM((2, tm, d), F32), pltpu.SemaphoreType.DMA(())],
        compiler_params=_cparams(("arbitrary",)),
        name="moe_combine",
    )(pos_t, h, wts, ys)


def _routing_tables(ids, n_experts, tm):
    n = ids.shape[0]
    flat = ids.reshape(-1)
    onehot = (flat[:, None] == jnp.arange(n_experts, dtype=jnp.int32)[None, :]).astype(jnp.int32)
    csum = jnp.cumsum(onehot, axis=0)
    counts = csum[-1]
    rank = jnp.take_along_axis(csum, flat[:, None], axis=1)[:, 0] - 1
    padded = ((counts + tm - 1) // tm) * tm
    ends = jnp.cumsum(padded)
    offs = ends - padded
    pos = offs[flat] + rank
    p = 2 * n + n_experts * tm
    p = (p // tm) * tm
    src = jnp.zeros((p,), jnp.int32).at[pos].set(jnp.arange(2 * n, dtype=jnp.int32) // 2)
    tile_start = jnp.arange(p // tm, dtype=jnp.int32) * tm
    tile_expert = jnp.minimum(jnp.searchsorted(ends, tile_start, side="right"), n_experts - 1).astype(jnp.int32)
    return pos.reshape(n, 2), src, tile_expert


def hier_moe_block(h, gain, w_group, b_group, w_expert, b_expert, w_gate_up, w_down, *, tm=256, fc=256):
    n_experts = w_expert.shape[1]
    xn, ids, wts = router(h, gain, w_group, b_group, w_expert, b_expert)
    pos, src, tile_expert = _routing_tables(ids[:, :2], n_experts, tm)
    xs = gather_rows(xn, src, tm)
    hs = moe_up(xs, tile_expert, w_gate_up, tm, fc)
    ys = moe_down(hs, tile_expert, w_down, tm)
    return moe_combine(h, wts, ys, pos)


def kernel(x, a_norm_g, a_w_in, a_w_decay_up, a_b_decay, a_out_norm_g, a_w_out, kv_norm_g, w_kv, k_norm_g,
           b_norm_g, b_w_q, b_q_norm_g, b_sinks, b_w_out, ffn_norm_g, moe_w_group, moe_b_group, moe_w_expert,
           moe_b_expert, moe_w_gate_up, moe_w_down):
    bsz, t, d = x.shape
    n = bsz * t
    h = x.reshape(n, d)
    tm_mm = 1024

    heads = GLA_HEADS
    dk_all = a_w_decay_up.shape[2]
    dk = dk_all // heads
    dv_all = a_w_out.shape[1]
    dv = dv_all // heads
    n_main = 2 * dk_all + 2 * dv_all
    (xn,) = rmsnorm(h, a_norm_g[0:1])
    w_in = a_w_in[0].astype(BF16)
    proj = matmul(xn, w_in, n_cols=n_main, tm=tm_mm, tn=512, out_dtype=BF16)
    w_z = jnp.zeros((d, LANES), BF16).at[:, :GLA_GATE_RANK].set(a_w_in[0][:, n_main:].astype(BF16))
    z = matmul(xn, w_z, n_cols=LANES, tm=tm_mm, tn=LANES, out_dtype=F32)
    gated = gla_core(proj, z, a_w_decay_up[0], a_b_decay[0], a_out_norm_g[0],
                     batch=bsz, heads=heads, dk=dk, dv=dv)
    h = matmul(gated, a_w_out[0].astype(BF16), n_cols=d, tm=tm_mm, tn=512, res=h)
    h = hier_moe_block(h, ffn_norm_g[0], moe_w_group[0], moe_b_group[0], moe_w_expert[0], moe_b_expert[0],
                       moe_w_gate_up[0], moe_w_down[0])

    hd = SWA_HEAD_DIM
    kvh = SWA_KV_HEADS
    hq = b_w_q.shape[2] // hd
    xn_kv, xn_b = rmsnorm(h, jnp.stack([kv_norm_g, b_norm_g[0]], axis=0))
    w_kv_b = w_kv.astype(BF16)
    kw = kvh * hd
    k_sh = matmul(xn_kv, w_kv_b, n_cols=kw, tm=tm_mm, tn=kw, out_dtype=BF16, head_dim=hd, head_gain=k_norm_g)
    v_sh = matmul(xn_kv, w_kv_b, n_cols=kw, col_block_off=1, tm=tm_mm, tn=kw, out_dtype=BF16)
    q = matmul(xn_b, b_w_q[0].astype(BF16), n_cols=hq * hd, tm=tm_mm, tn=512, out_dtype=BF16,
               head_dim=hd, head_gain=b_q_norm_g[0], scale=hd ** -0.5)
    to_heads = lambda a, nh: a.reshape(bsz, t, nh, hd).transpose(0, 2, 1, 3)
    o = swa_attention(to_heads(q, hq), to_heads(k_sh, kvh), to_heads(v_sh, kvh), b_sinks[0])
    o = o.transpose(0, 2, 1, 3).reshape(n, hq * hd)
    h = matmul(o, b_w_out[0].astype(BF16), n_cols=d, tm=tm_mm, tn=512, res=h)
    h = hier_moe_block(h, ffn_norm_g[1], moe_w_group[1], moe_b_group[1], moe_w_expert[1], moe_b_expert[1],
                       moe_w_gate_up[1], moe_w_down[1])
    return h.reshape(bsz, t, d)
```

```python
import functools
import math

import numpy as np
import jax
import jax.numpy as jnp
from jax import lax
from jax.experimental import pallas as pl
from jax.experimental.pallas import tpu as pltpu

F32 = jnp.float32
BF16 = jnp.bfloat16
U32 = jnp.uint32
EPS = 1e-6

GLA_HEADS = 8
GLA_GATE_RANK = 16
GLA_GATE_TAU = 16.0
SWA_HEAD_DIM = 64
SWA_KV_HEADS = 8
SWA_WINDOW = 128

LANES = 128
SUBLANES = 8
VMEM_LIMIT_BYTES = 56 * 1024 * 1024

GLA_CHUNK = 64
NEG_BIG = -1e30
MOE_TILE = 256


def _cparams(sem):
    return pltpu.CompilerParams(dimension_semantics=sem, vmem_limit_bytes=VMEM_LIMIT_BYTES)


def _split3(x):
    hi = x.astype(BF16)
    r1 = x - hi.astype(F32)
    mid = r1.astype(BF16)
    lo = (r1 - mid.astype(F32)).astype(BF16)
    return hi, mid, lo


def _dot(a, b):
    return jnp.dot(a, b, preferred_element_type=F32)


def _dot_nt(a, b):
    return lax.dot_general(a, b, (((1,), (1,)), ((), ())), preferred_element_type=F32)


def _dot_tn(a, b):
    return lax.dot_general(a, b, (((0,), (0,)), ((), ())), preferred_element_type=F32)


def _rmsnorm_body(x_ref, g_ref, *o_refs):
    x = x_ref[...]
    y = x * lax.rsqrt(jnp.mean(x * x, axis=-1, keepdims=True) + EPS)
    for i, o_ref in enumerate(o_refs):
        o_ref[...] = (y * g_ref[i:i + 1, :]).astype(o_ref.dtype)


def rmsnorm(x, gains, out_dtype=BF16, tm=256):
    n, d = x.shape
    k = gains.shape[0]
    outs = pl.pallas_call(
        _rmsnorm_body,
        grid=(n // tm,),
        in_specs=[pl.BlockSpec((tm, d), lambda i: (i, 0)), pl.BlockSpec((k, d), lambda i: (0, 0))],
        out_specs=[pl.BlockSpec((tm, d), lambda i: (i, 0)) for _ in range(k)],
        out_shape=[jax.ShapeDtypeStruct((n, d), out_dtype) for _ in range(k)],
        compiler_params=_cparams(("parallel",)),
        name="rmsnorm",
    )(x, gains.astype(F32))
    return outs


def _mm_body(*refs, has_res, head_dim, scale, head_major):
    a_ref, w_ref = refs[0], refs[1]
    pos = 2
    res_ref = None
    if has_res:
        res_ref = refs[pos]
        pos += 1
    if head_dim:
        gain_ref, ind_ref, indt_ref = refs[pos:pos + 3]
        pos += 3
    o_ref = refs[pos]
    y = _dot(a_ref[...], w_ref[...])
    if head_dim:
        ssq = _dot((y * y).astype(BF16), ind_ref[...])
        inv = lax.rsqrt(ssq * (1.0 / head_dim) + EPS)
        hi, mid, _ = _split3(inv)
        it = indt_ref[...]
        y = y * (_dot(hi, it) + _dot(mid, it)) * (gain_ref[...] * scale)
    if has_res:
        y = y + res_ref[...]
    if head_major:
        hd = o_ref.shape[-1]
        for hh in range(o_ref.shape[0]):
            o_ref[hh] = y[:, hh * hd:(hh + 1) * hd].astype(o_ref.dtype)
    else:
        o_ref[...] = y.astype(o_ref.dtype)


def matmul(a, w, *, n_cols, col_block_off=0, tm, tn, res=None, out_dtype=F32,
           head_dim=0, head_gain=None, scale=1.0, head_major=0):
    n, k = a.shape
    assert n % tm == 0 and n_cols % tn == 0
    in_specs = [pl.BlockSpec((tm, k), lambda i, j: (i, 0)),
                pl.BlockSpec((k, tn), lambda i, j: (0, j + col_block_off))]
    args = [a, w]
    if res is not None:
        in_specs.append(pl.BlockSpec((tm, tn), lambda i, j: (i, j)))
        args.append(res)
    if head_dim:
        assert tn % head_dim == 0 and tn // head_dim <= LANES
        nh = tn // head_dim
        ind = np.zeros((tn, LANES), np.float32)
        ind[np.arange(tn), np.arange(tn) // head_dim] = 1.0
        gain_row = jnp.tile(head_gain.astype(F32), nh).reshape(1, tn)
        in_specs += [pl.BlockSpec((1, tn), lambda i, j: (0, 0)),
                     pl.BlockSpec((tn, LANES), lambda i, j: (0, 0)),
                     pl.BlockSpec((LANES, tn), lambda i, j: (0, 0))]
        args += [gain_row, jnp.asarray(ind, BF16), jnp.asarray(ind.T, BF16)]
    if head_major:
        assert tn % head_major == 0
        out_spec = pl.BlockSpec((tn // head_major, tm, head_major), lambda i, j: (j, i, 0))
        out_shape = jax.ShapeDtypeStruct((n_cols // head_major, n, head_major), out_dtype)
    else:
        out_spec = pl.BlockSpec((tm, tn), lambda i, j: (i, j))
        out_shape = jax.ShapeDtypeStruct((n, n_cols), out_dtype)
    return pl.pallas_call(
        functools.partial(_mm_body, has_res=res is not None, head_dim=head_dim, scale=scale,
                          head_major=head_major),
        grid=(n // tm, n_cols // tn),
        in_specs=in_specs,
        out_specs=out_spec,
        out_shape=out_shape,
        compiler_params=_cparams(("parallel", "arbitrary")),
        name="matmul",
    )(*args)


def _gla_tables(c):
    levels = []
    l = c // 2
    while l >= 1:
        levels.append(l)
        l //= 2
    rows = np.arange(c)
    wcum = (rows[None, :] <= rows[:, None]).astype(np.float32)
    wlev = []
    masks = [np.eye(c, dtype=np.float32)]
    for l in levels:
        blk = rows // l
        odd = (blk % 2) == 1
        ref = np.where(odd, blk * l, (blk + 1) * l)
        lo = np.minimum(rows, ref)
        hi = np.maximum(rows, ref)
        wlev.append(((rows[None, :] > lo[:, None]) & (rows[None, :] <= hi[:, None])).astype(np.float32))
        masks.append((odd[:, None] & (blk[None, :] == blk[:, None] - 1)).astype(np.float32))
    return levels, wcum, np.concatenate(wlev, axis=0), np.stack(masks, axis=0)


def _gla_body(q_ref, k_ref, v_ref, r_ref, z_ref, wup_ref, bd_ref, gout_ref, wcum_ref, wlev_ref, mask_ref,
              o_ref, st_ref, e_ref, x_ref, qe_ref, kr_ref, acc_ref, upd_ref, *, c, levels, scale, tau):
    tb, dk = q_ref.shape
    nch = tb // c
    nlev = len(levels)

    @pl.when(pl.program_id(2) == 0)
    def _():
        st_ref[...] = jnp.zeros_like(st_ref)

    zh, zm, _ = _split3(z_ref[...])
    wh, wm = wup_ref[0], wup_ref[1]
    pre = _dot(zh, wh) + _dot(zh, wm) + _dot(zm, wh) + bd_ref[...]
    g = (jnp.minimum(pre, 0.0) - jnp.log(1.0 + jnp.exp(-jnp.abs(pre)))) * (1.0 / tau)
    g_hi = g.astype(BF16)
    g_lo = (g - g_hi.astype(F32)).astype(BF16)
    for ci in range(nch):
        rows = slice(ci * c, (ci + 1) * c)
        ec = _dot(wcum_ref[...], jnp.concatenate([g_hi[rows], g_lo[rows]], axis=1))
        e_ref[0, rows, :] = ec[:, :dk] + ec[:, dk:]
        el = _dot(wlev_ref[...], g_hi[rows])
        for li in range(nlev):
            e_ref[li + 1, rows, :] = el[li * c:(li + 1) * c]
    q = q_ref[...].astype(F32) * scale
    k = k_ref[...].astype(F32)
    ecum = e_ref[0]
    qe_ref[...] = (q * jnp.exp(ecum)).astype(BF16)
    lasts = [ecum[(ci + 1) * c - 1:(ci + 1) * c, :] for ci in range(nch)]
    last_b = jnp.concatenate([jnp.broadcast_to(l, (c, dk)) for l in lasts], axis=0)
    kr_ref[...] = (k * jnp.exp(last_b - ecum)).astype(BF16)
    row = lax.broadcasted_iota(jnp.int32, (tb, dk), 0)
    for li, l in enumerate(levels):
        odd = ((row >> int(math.log2(l))) & 1) == 1
        x_ref[li] = (jnp.where(odd, q, k) * jnp.exp(e_ref[li + 1])).astype(BF16)
    qb = q.astype(BF16)
    kb = k.astype(BF16)
    for ci in range(nch):
        rows = slice(ci * c, (ci + 1) * c)
        a = mask_ref[0] * _dot_nt(qb[rows], kb[rows])
        for li in range(nlev):
            xc = x_ref[li, rows, :]
            a = a + mask_ref[li + 1] * _dot_nt(xc, xc)
        v = v_ref[rows, :]
        acc_ref[rows, :] = _dot(a.astype(BF16), v)
        upd_ref[ci] = _dot_tn(v, kr_ref[rows, :])
    st = st_ref[...]
    for ci in range(nch):
        rows = slice(ci * c, (ci + 1) * c)
        acc_ref[rows, :] = acc_ref[rows, :] + _dot_nt(qe_ref[rows, :], st.astype(BF16))
        st = st * jnp.exp(lasts[ci]) + upd_ref[ci]
    st_ref[...] = st
    o = acc_ref[...]
    o = o * lax.rsqrt(jnp.mean(o * o, axis=-1, keepdims=True) + EPS) * gout_ref[...]
    r = r_ref[...].astype(F32)
    o_ref[...] = (o * (r / (1.0 + jnp.exp(-r)))).astype(o_ref.dtype)


def gla_core(proj, z, w_up, b_decay, g_out, *, batch, heads, dk, dv, tb=512, c=GLA_CHUNK):
    n = proj.shape[0]
    t = n // batch
    nb = t // tb
    levels, wcum, wlev, masks = _gla_tables(c)
    nlev = len(levels)
    rank = w_up.shape[0]
    wup_pad = jnp.zeros((LANES, heads * dk), F32).at[:rank].set(w_up.astype(F32))
    wup2 = jnp.stack(_split3(wup_pad)[:2], axis=0)
    kv0 = 2 * heads * dk // dv
    row_map = lambda b, h, i: b * nb + i
    const2 = lambda b, h, i: (0, 0)
    return pl.pallas_call(
        functools.partial(_gla_body, c=c, levels=levels, scale=dk ** -0.5, tau=GLA_GATE_TAU),
        grid=(batch, heads, nb),
        in_specs=[
            pl.BlockSpec((tb, dk), lambda b, h, i: (row_map(b, h, i), h)),
            pl.BlockSpec((tb, dk), lambda b, h, i: (row_map(b, h, i), heads + h)),
            pl.BlockSpec((tb, dv), lambda b, h, i: (row_map(b, h, i), kv0 + h)),
            pl.BlockSpec((tb, dv), lambda b, h, i: (row_map(b, h, i), kv0 + heads + h)),
            pl.BlockSpec((tb, LANES), lambda b, h, i: (row_map(b, h, i), 0)),
            pl.BlockSpec((2, LANES, dk), lambda b, h, i: (0, 0, h)),
            pl.BlockSpec((1, dk), lambda b, h, i: (0, h)),
            pl.BlockSpec((1, dv), const2),
            pl.BlockSpec(wcum.shape, const2),
            pl.BlockSpec(wlev.shape, const2),
            pl.BlockSpec(masks.shape, lambda b, h, i: (0, 0, 0)),
        ],
        out_specs=pl.BlockSpec((tb, dv), lambda b, h, i: (row_map(b, h, i), h)),
        out_shape=jax.ShapeDtypeStruct((n, heads * dv), BF16),
        scratch_shapes=[pltpu.VMEM((dv, dk), F32),
                        pltpu.VMEM((nlev + 1, tb, dk), F32),
                        pltpu.VMEM((nlev, tb, dk), BF16),
                        pltpu.VMEM((tb, dk), BF16),
                        pltpu.VMEM((tb, dk), BF16),
                        pltpu.VMEM((tb, dv), F32),
                        pltpu.VMEM((tb // c, dv, dk), F32)],
        compiler_params=_cparams(("parallel", "parallel", "arbitrary")),
        name="gla_core",
    )(proj, proj, proj, proj, z, wup2, b_decay.reshape(1, -1).astype(F32), g_out.reshape(1, -1).astype(F32),
      jnp.asarray(wcum, BF16), jnp.asarray(wlev, BF16), jnp.asarray(masks, F32))


def _swa_body(q_ref, kc_ref, kp_ref, vc_ref, vp_ref, sink_ref, bias_ref, o_ref, *, group, blk):
    kvh, _, hd = kc_ref.shape
    gw = group * hd
    for h in range(kvh):
        qh = q_ref[:, h * gw:(h + 1) * gw]
        q_all = jnp.concatenate([qh[:, g * hd:(g + 1) * hd] for g in range(group)], axis=0)
        k2 = jnp.concatenate([kp_ref[h], kc_ref[h]], axis=0)
        v2 = jnp.concatenate([vp_ref[h], vc_ref[h]], axis=0)
        s = _dot_nt(k2, q_all) + bias_ref[...]
        sink = sink_ref[h]
        m = jnp.maximum(jnp.max(s, axis=0, keepdims=True), sink)
        p = jnp.exp(s - m)
        denom = jnp.sum(p, axis=0, keepdims=True) + jnp.exp(sink - m)
        o_t = _dot_tn(v2, p.astype(BF16)) * (1.0 / denom)
        o = jnp.concatenate([o_t[:, g * blk:(g + 1) * blk].T for g in range(group)], axis=1)
        o_ref[:, h * gw:(h + 1) * gw] = o.astype(o_ref.dtype)


def swa_attention(q, k, v, sinks, *, batch, blk=SWA_WINDOW):
    n, qw = q.shape
    kvh, _, hd = k.shape
    group = qw // hd // kvh
    nb = n // batch // blk
    ki = np.arange(2 * blk)[:, None]
    qi = np.tile(np.arange(blk), group)[None, :]
    dist = blk + qi - ki
    band = (dist >= 0) & (dist < blk)
    bias = np.stack([np.where(band & (ki >= blk), 0.0, NEG_BIG), np.where(band, 0.0, NEG_BIG)], axis=0)
    sink_rows = jnp.repeat(sinks.astype(F32).reshape(kvh, group), blk, axis=1).reshape(kvh, 1, group * blk)
    row = lambda b, j: b * nb + j
    prev = lambda b, j: (0, b * nb + jnp.maximum(j - 1, 0), 0)
    cur = lambda b, j: (0, row(b, j), 0)
    kv_c = pl.BlockSpec((kvh, blk, hd), cur)
    kv_p = pl.BlockSpec((kvh, blk, hd), prev)
    return pl.pallas_call(
        functools.partial(_swa_body, group=group, blk=blk),
        grid=(batch, nb),
        in_specs=[
            pl.BlockSpec((blk, qw), lambda b, j: (row(b, j), 0)),
            kv_c, kv_p, kv_c, kv_p,
            pl.BlockSpec((kvh, 1, group * blk), lambda b, j: (0, 0, 0)),
            pl.BlockSpec((None, 2 * blk, group * blk), lambda b, j: (jnp.minimum(j, 1), 0, 0)),
        ],
        out_specs=pl.BlockSpec((blk, qw), lambda b, j: (row(b, j), 0)),
        out_shape=jax.ShapeDtypeStruct((n, qw), BF16),
        compiler_params=_cparams(("parallel", "arbitrary")),
        name="swa_attention",
    )(q, k, k, v, v, sink_rows, jnp.asarray(bias.astype(np.float32)))


def _pack_rows(y, o_ref):
    rows, d = y.shape
    ns = _slab_rows(d)
    for s in range(ns):
        lo = y[:, (2 * s) * LANES:(2 * s + 1) * LANES]
        hi = y[:, (2 * s + 1) * LANES:(2 * s + 2) * LANES]
        lo_bits = lax.bitcast_convert_type(lo.astype(BF16).astype(F32), U32)
        hi_bits = lax.bitcast_convert_type(hi.astype(BF16).astype(F32), U32)
        o_ref[pl.ds(s, rows, stride=ns), :] = (lo_bits >> 16) | (hi_bits & jnp.uint32(0xFFFF0000))


def _slab_rows(d):
    assert d % (2 * SUBLANES * LANES) == 0
    return d // (2 * LANES)


def _unpack_rows(x_ref, s, ns):
    w = x_ref[pl.ds(s, x_ref.shape[0] // ns, stride=ns), :]
    lo = lax.bitcast_convert_type(w << 16, F32)
    hi = lax.bitcast_convert_type(w & jnp.uint32(0xFFFF0000), F32)
    return lo, hi


def _router_body(x_ref, g_ref, wr_ref, br_ref, tri_ref, xn_ref, meta_ref, wts_ref, cnt_ref, carry_ref,
                 *, groups, per_group):
    @pl.when(pl.program_id(0) == 0)
    def _():
        carry_ref[...] = jnp.zeros_like(carry_ref)

    x = x_ref[...]
    xn = x * lax.rsqrt(jnp.mean(x * x, axis=-1, keepdims=True) + EPS) * g_ref[...]
    _pack_rows(xn, xn_ref)
    xh, xm, _ = _split3(xn)
    wh, wm = wr_ref[0], wr_ref[1]
    logits = _dot(xh, wh) + _dot(xh, wm) + _dot(xm, wh) + br_ref[...]
    lane = lax.broadcasted_iota(jnp.int32, logits.shape, 1).astype(F32)
    big = jnp.float32(LANES)
    neg = jnp.float32(-jnp.inf)
    gl = jnp.where(lane < groups, logits, neg)
    gm = jnp.max(gl, axis=-1, keepdims=True)
    p_top = 1.0 / jnp.sum(jnp.exp(gl - gm), axis=-1, keepdims=True)
    g_idx = jnp.min(jnp.where(gl == gm, lane, big), axis=-1, keepdims=True)
    lo = groups + g_idx * per_group
    el = jnp.where((lane >= lo) & (lane < lo + per_group), logits, neg)
    v1 = jnp.max(el, axis=-1, keepdims=True)
    i1 = jnp.min(jnp.where(el == v1, lane, big), axis=-1, keepdims=True)
    el2 = jnp.where(lane == i1, neg, el)
    v2 = jnp.max(el2, axis=-1, keepdims=True)
    i2 = jnp.min(jnp.where(el2 == v2, lane, big), axis=-1, keepdims=True)
    e21 = jnp.exp(v2 - v1)
    w1 = p_top / (1.0 + e21)
    w2 = p_top * e21 / (1.0 + e21)
    sel = jnp.where((lane == i1) | (lane == i2), 1.0, 0.0)
    before = _dot(tri_ref[...], sel.astype(BF16)) + carry_ref[...]
    r1 = jnp.sum(jnp.where(lane == i1, before, 0.0), axis=-1, keepdims=True)
    r2 = jnp.sum(jnp.where(lane == i2, before, 0.0), axis=-1, keepdims=True)
    carry = carry_ref[...] + jnp.sum(sel, axis=0, keepdims=True)
    carry_ref[...] = carry
    cnt_ref[...] = jnp.broadcast_to(carry, cnt_ref.shape)
    meta = jnp.where(lane == 0, i1 - groups, jnp.where(lane == 1, i2 - groups,
                     jnp.where(lane == 2, r1, jnp.where(lane == 3, r2, 0.0))))
    meta_ref[...] = meta.astype(jnp.int32)
    wts_ref[...] = jnp.where(lane == 0, w1, jnp.where(lane == 1, w2, 0.0))


def router(h, gain, w_group, b_group, w_expert, b_expert, tm=256):
    n, d = h.shape
    ns = _slab_rows(d)
    groups = w_group.shape[1]
    experts = w_expert.shape[1]
    wr = jnp.zeros((d, LANES), F32).at[:, :groups].set(w_group).at[:, groups:groups + experts].set(w_expert)
    br = jnp.zeros((1, LANES), F32).at[0, :groups].set(b_group).at[0, groups:groups + experts].set(b_expert)
    wr2 = jnp.stack(_split3(wr)[:2], axis=0)
    tri = jnp.asarray(np.tril(np.ones((tm, tm), np.float32), -1), BF16)
    row = lambda i: (i, 0)
    xn, meta, wts, cnt = pl.pallas_call(
        functools.partial(_router_body, groups=groups, per_group=experts // groups),
        grid=(n // tm,),
        in_specs=[pl.BlockSpec((tm, d), row), pl.BlockSpec((1, d), lambda i: (0, 0)),
                  pl.BlockSpec((2, d, LANES), lambda i: (0, 0, 0)), pl.BlockSpec((1, LANES), lambda i: (0, 0)),
                  pl.BlockSpec((tm, tm), lambda i: (0, 0))],
        out_specs=[pl.BlockSpec((tm * ns, LANES), row),
                   pl.BlockSpec((tm, LANES), row), pl.BlockSpec((tm, LANES), row),
                   pl.BlockSpec((SUBLANES, LANES), lambda i: (0, 0))],
        out_shape=[jax.ShapeDtypeStruct((n * ns, LANES), U32),
                   jax.ShapeDtypeStruct((n, LANES), jnp.int32),
                   jax.ShapeDtypeStruct((n, LANES), F32),
                   jax.ShapeDtypeStruct((SUBLANES, LANES), F32)],
        scratch_shapes=[pltpu.VMEM((1, LANES), F32)],
        compiler_params=_cparams(("arbitrary",)),
        name="moe_router",
    )(h, gain.reshape(1, d).astype(F32), wr2, br, tri)
    counts = cnt[0, groups:groups + experts].astype(jnp.int32)
    return xn, meta[:, 0:2], meta[:, 2:4], wts, counts


def _row_copy(src_ref, src_row, dst_ref, dst_row, sem, ns):
    src = src_ref.at[pl.ds(pl.multiple_of(src_row * ns, ns), ns)]
    dst = dst_ref.at[pl.ds(pl.multiple_of(dst_row * ns, ns), ns)]
    return pltpu.make_async_copy(src, dst, sem)


def _pad_bits(tm):
    return [1 << b for b in range(int(math.log2(tm)))]


def _dispatch_body(pad_start_ref, pad_count_ref, nu_ref, pos_ref, x_hbm, o_hbm, zero_ref, sem, *, tm, tile,
                   n_experts, ns):
    t = pl.program_id(0)
    half = tile // 2

    @pl.when(t == 0)
    def _():
        zero_ref[...] = jnp.zeros_like(zero_ref)

        def zero_tile(i, c):
            cps = [pltpu.make_async_copy(
                zero_ref, o_hbm.at[pl.ds(pl.multiple_of((i * tile + k * half) * ns, ns), half * ns)], sem)
                for k in range(2)]
            for cp in cps:
                cp.start()
            for cp in cps:
                cp.wait()
            return c
        lax.fori_loop(nu_ref[0], o_hbm.shape[0] // (tile * ns), zero_tile, 0)
        for phase in ("start", "wait"):
            for e in range(n_experts):
                cnt = pad_count_ref[e]
                off = pad_start_ref[e]
                for bit in _pad_bits(tile):
                    dst0 = pl.multiple_of(off * ns, ns)
                    cp = pltpu.make_async_copy(zero_ref.at[pl.ds(0, bit * ns)],
                                               o_hbm.at[pl.ds(dst0, bit * ns)], sem)

                    @pl.when((cnt & bit) != 0)
                    def _():
                        cp.start() if phase == "start" else cp.wait()
                    off = off + (cnt & bit)

    def start(r, c):
        tok = t * tm + r
        _row_copy(x_hbm, tok, o_hbm, pos_ref[0, 0, r], sem, ns).start()
        _row_copy(x_hbm, tok, o_hbm, pos_ref[0, 0, tm + r], sem, ns).start()
        return c
    lax.fori_loop(0, tm, start, 0)

    def wait(r, c):
        _row_copy(x_hbm, 0, o_hbm, 0, sem, ns).wait()
        _row_copy(x_hbm, 0, o_hbm, 0, sem, ns).wait()
        return c
    lax.fori_loop(0, tm, wait, 0)


def moe_dispatch(xn, pos, pad_start, pad_count, n_used, p_rows, tile, ns, tm=256):
    n = xn.shape[0] // ns
    nt = n // tm
    n_experts = pad_start.shape[0]
    pos_t = pos.reshape(nt, tm, 2).transpose(0, 2, 1).reshape(nt, 1, 2 * tm)
    return pl.pallas_call(
        functools.partial(_dispatch_body, tm=tm, tile=tile, n_experts=n_experts, ns=ns),
        grid_spec=pltpu.PrefetchScalarGridSpec(
            num_scalar_prefetch=3,
            grid=(nt,),
            in_specs=[pl.BlockSpec((1, 1, 2 * tm), lambda t, ps, pc, nu: (t, 0, 0), memory_space=pltpu.SMEM),
                      pl.BlockSpec(memory_space=pl.ANY)],
            out_specs=pl.BlockSpec(memory_space=pl.ANY),
            scratch_shapes=[pltpu.VMEM((tile // 2 * ns, LANES), xn.dtype), pltpu.SemaphoreType.DMA(())],
        ),
        out_shape=jax.ShapeDtypeStruct((p_rows * ns, LANES), xn.dtype),
        compiler_params=_cparams(("arbitrary",)),
        name="moe_dispatch",
    )(pad_start, pad_count, n_used, pos_t, xn)


def _moe_up_body(te_ref, nu_ref, x_ref, wg_ref, wu_ref, o_ref, wgb_ref, wub_ref):
    t = pl.program_id(1)
    changed = te_ref[t] != te_ref[jnp.maximum(t - 1, 0)]

    @pl.when((t < nu_ref[0]) & ((t == 0) | changed))
    def _():
        wgb_ref[...] = wg_ref[...].astype(BF16)
        wub_ref[...] = wu_ref[...].astype(BF16)

    @pl.when(t < nu_ref[0])
    def _():
        ns = _slab_rows(wgb_ref.shape[0])
        x = jnp.concatenate([xs.astype(BF16) for s in range(ns) for xs in _unpack_rows(x_ref, s, ns)], axis=1)
        a = _dot(x, wgb_ref[...])
        u = _dot(x, wub_ref[...])
        o_ref[...] = (a / (1.0 + jnp.exp(-a)) * u).astype(o_ref.dtype)

    @pl.when(t >= nu_ref[0])
    def _():
        o_ref[...] = jnp.zeros_like(o_ref)


def moe_up(xs, tile_expert, n_used, w_gate_up, layer, tm, fc):
    d, f2 = w_gate_up.shape[2:]
    ns = _slab_rows(d)
    p = xs.shape[0] // ns
    f = f2 // 2
    nc = f // fc
    tile = lambda t, nu: jnp.minimum(t, nu[0] - 1)
    return pl.pallas_call(
        _moe_up_body,
        grid_spec=pltpu.PrefetchScalarGridSpec(
            num_scalar_prefetch=2,
            grid=(nc, p // tm),
            in_specs=[pl.BlockSpec((tm * ns, LANES), lambda c, t, te, nu: (tile(t, nu), 0)),
                      pl.BlockSpec((None, None, d, fc), lambda c, t, te, nu: (layer, te[tile(t, nu)], 0, c)),
                      pl.BlockSpec((None, None, d, fc), lambda c, t, te, nu: (layer, te[tile(t, nu)], 0, nc + c))],
            out_specs=pl.BlockSpec((tm, fc), lambda c, t, te, nu: (t, c)),
            scratch_shapes=[pltpu.VMEM((d, fc), BF16), pltpu.VMEM((d, fc), BF16)],
        ),
        out_shape=jax.ShapeDtypeStruct((p, f), BF16),
        compiler_params=_cparams(("arbitrary", "arbitrary")),
        name="moe_up",
    )(tile_expert, n_used, xs, w_gate_up, w_gate_up)


def _moe_down_body(te_ref, nu_ref, h_ref, w_ref, o_ref, wb_ref):
    t = pl.program_id(0)
    changed = te_ref[t] != te_ref[jnp.maximum(t - 1, 0)]

    @pl.when((t < nu_ref[0]) & ((t == 0) | changed))
    def _():
        wb_ref[...] = w_ref[...].astype(BF16)

    @pl.when(t < nu_ref[0])
    def _():
        _pack_rows(_dot(h_ref[...], wb_ref[...]), o_ref)

    @pl.when(t >= nu_ref[0])
    def _():
        o_ref[...] = jnp.zeros_like(o_ref)


def moe_down(hs, tile_expert, n_used, w_down, layer, tm):
    p, f = hs.shape
    d = w_down.shape[3]
    tile = lambda t, nu: jnp.minimum(t, nu[0] - 1)
    return pl.pallas_call(
        _moe_down_body,
        grid_spec=pltpu.PrefetchScalarGridSpec(
            num_scalar_prefetch=2,
            grid=(p // tm,),
            in_specs=[pl.BlockSpec((tm, f), lambda t, te, nu: (tile(t, nu), 0)),
                      pl.BlockSpec((None, None, f, d), lambda t, te, nu: (layer, te[tile(t, nu)], 0, 0))],
            out_specs=pl.BlockSpec((tm * _slab_rows(d), LANES), lambda t, te, nu: (t, 0)),
            scratch_shapes=[pltpu.VMEM((f, d), BF16)],
        ),
        out_shape=jax.ShapeDtypeStruct((p * _slab_rows(d), LANES), U32),
        compiler_params=_cparams(("arbitrary",)),
        name="moe_down",
    )(tile_expert, n_used, hs, w_down)


def _combine_body(pos_ref, h_ref, w_ref, y_hbm, o_ref, buf_ref, sem, *, tm):
    ns = _slab_rows(h_ref.shape[1])

    def start(r, c):
        _row_copy(y_hbm, pos_ref[0, 0, r], buf_ref.at[0], r, sem, ns).start()
        _row_copy(y_hbm, pos_ref[0, 0, tm + r], buf_ref.at[1], r, sem, ns).start()
        return c
    lax.fori_loop(0, tm, start, 0)

    def wait(r, c):
        _row_copy(y_hbm, 0, buf_ref.at[0], r, sem, ns).wait()
        _row_copy(y_hbm, 0, buf_ref.at[1], r, sem, ns).wait()
        return c
    lax.fori_loop(0, tm, wait, 0)
    w = w_ref[...]
    w0 = w[:, 0:1]
    w1 = w[:, 1:2]
    for s in range(ns):
        a_lo, a_hi = _unpack_rows(buf_ref.at[0], s, ns)
        b_lo, b_hi = _unpack_rows(buf_ref.at[1], s, ns)
        c_lo = slice((2 * s) * LANES, (2 * s + 1) * LANES)
        c_hi = slice((2 * s + 1) * LANES, (2 * s + 2) * LANES)
        o_ref[:, c_lo] = h_ref[:, c_lo] + w0 * a_lo + w1 * b_lo
        o_ref[:, c_hi] = h_ref[:, c_hi] + w0 * a_hi + w1 * b_hi


def moe_combine(h, wts, ys, pos, tm=256):
    n, d = h.shape
    nt = n // tm
    pos_t = pos.reshape(nt, tm, 2).transpose(0, 2, 1).reshape(nt, 1, 2 * tm)
    return pl.pallas_call(
        functools.partial(_combine_body, tm=tm),
        grid=(nt,),
        in_specs=[pl.BlockSpec((1, 1, 2 * tm), lambda t: (t, 0, 0), memory_space=pltpu.SMEM),
                  pl.BlockSpec((tm, d), lambda t: (t, 0)),
                  pl.BlockSpec((tm, LANES), lambda t: (t, 0)),
                  pl.BlockSpec(memory_space=pl.ANY)],
        out_specs=pl.BlockSpec((tm, d), lambda t: (t, 0)),
        out_shape=jax.ShapeDtypeStruct((n, d), F32),
        scratch_shapes=[pltpu.VMEM((2, tm * _slab_rows(d), LANES), ys.dtype), pltpu.SemaphoreType.DMA(())],
        compiler_params=_cparams(("arbitrary",)),
        name="moe_combine",
    )(pos_t, h, wts, ys)


def _routing_tables(ids, ranks, counts, tm):
    n = ids.shape[0]
    n_experts = counts.shape[0]
    padded = ((counts + tm - 1) // tm) * tm
    ends = jnp.cumsum(padded)
    offs = ends - padded
    experts = jnp.arange(n_experts, dtype=jnp.int32)
    pos = ranks + jnp.sum(jnp.where(ids[:, :, None] == experts[None, None, :], offs[None, None, :], 0), axis=-1)
    p_rows = ((2 * n + n_experts * (tm - 1)) // tm) * tm
    tile_start = jnp.arange(p_rows // tm, dtype=jnp.int32) * tm
    tile_expert = jnp.minimum(jnp.sum((ends[None, :] <= tile_start[:, None]).astype(jnp.int32), axis=1),
                              n_experts - 1)
    n_used = (ends[-1:] // tm).astype(jnp.int32)
    return pos.astype(jnp.int32), tile_expert, n_used, (offs + counts).astype(jnp.int32), \
        (padded - counts).astype(jnp.int32), p_rows


def hier_moe_block(h, gain, w_group, b_group, w_expert, b_expert, w_gate_up, w_down, layer, *,
                   tm=MOE_TILE, fc=256):
    xn, ids, ranks, wts, counts = router(h, gain, w_group, b_group, w_expert, b_expert)
    pos, tile_expert, n_used, pad_start, pad_count, p_rows = _routing_tables(ids, ranks, counts, tm)
    xs = moe_dispatch(xn, pos, pad_start, pad_count, n_used, p_rows, tm, _slab_rows(h.shape[1]))
    hs = moe_up(xs, tile_expert, n_used, w_gate_up, layer, tm, fc)
    ys = moe_down(hs, tile_expert, n_used, w_down, layer, tm)
    return moe_combine(h, wts, ys, pos)


def kernel(x, a_norm_g, a_w_in, a_w_decay_up, a_b_decay, a_out_norm_g, a_w_out, kv_norm_g, w_kv, k_norm_g,
           b_norm_g, b_w_q, b_q_norm_g, b_sinks, b_w_out, ffn_norm_g, moe_w_group, moe_b_group, moe_w_expert,
           moe_b_expert, moe_w_gate_up, moe_w_down):
    bsz, t, d = x.shape
    n = bsz * t
    h = x.reshape(n, d)
    tm_mm = 1024

    def moe(h, layer):
        return hier_moe_block(h, ffn_norm_g[layer], moe_w_group[layer], moe_b_group[layer],
                              moe_w_expert[layer], moe_b_expert[layer], moe_w_gate_up, moe_w_down, layer)

    heads = GLA_HEADS
    dk_all = a_w_decay_up.shape[2]
    dk = dk_all // heads
    dv_all = a_w_out.shape[1]
    dv = dv_all // heads
    n_main = 2 * dk_all + 2 * dv_all
    (xn,) = rmsnorm(h, a_norm_g[0:1])
    w_in = a_w_in[0].astype(BF16)
    proj = matmul(xn, w_in, n_cols=n_main, tm=tm_mm, tn=1024, out_dtype=BF16)
    w_z = jnp.zeros((d, LANES), BF16).at[:, :GLA_GATE_RANK].set(a_w_in[0][:, n_main:].astype(BF16))
    z = matmul(xn, w_z, n_cols=LANES, tm=tm_mm, tn=LANES, out_dtype=F32)
    gated = gla_core(proj, z, a_w_decay_up[0], a_b_decay[0], a_out_norm_g[0],
                     batch=bsz, heads=heads, dk=dk, dv=dv)
    h = matmul(gated, a_w_out[0].astype(BF16), n_cols=d, tm=tm_mm, tn=512, res=h)
    h = moe(h, 0)

    hd = SWA_HEAD_DIM
    kvh = SWA_KV_HEADS
    hq = b_w_q.shape[2] // hd
    xn_kv, xn_b = rmsnorm(h, jnp.stack([kv_norm_g, b_norm_g[0]], axis=0))
    w_kv_b = w_kv.astype(BF16)
    kw = kvh * hd
    k_sh = matmul(xn_kv, w_kv_b, n_cols=kw, tm=tm_mm, tn=kw, out_dtype=BF16, head_dim=hd, head_gain=k_norm_g,
                  head_major=hd)
    v_sh = matmul(xn_kv, w_kv_b, n_cols=kw, col_block_off=1, tm=tm_mm, tn=kw, out_dtype=BF16, head_major=hd)
    q = matmul(xn_b, b_w_q[0].astype(BF16), n_cols=hq * hd, tm=tm_mm, tn=1024, out_dtype=BF16,
               head_dim=hd, head_gain=b_q_norm_g[0], scale=hd ** -0.5)
    o = swa_attention(q, k_sh, v_sh, b_sinks[0], batch=bsz)
    h = matmul(o, b_w_out[0].astype(BF16), n_cols=d, tm=tm_mm, tn=512, res=h)
    h = moe(h, 1)
    return h.reshape(bsz, t, d)
```

```python
import functools
import math

import numpy as np
import jax
import jax.numpy as jnp
from jax import lax
from jax.experimental import pallas as pl
from jax.experimental.pallas import tpu as pltpu

F32 = jnp.float32
BF16 = jnp.bfloat16
U32 = jnp.uint32
EPS = 1e-6

GLA_HEADS = 8
GLA_GATE_RANK = 16
GLA_GATE_TAU = 16.0
SWA_HEAD_DIM = 64
SWA_KV_HEADS = 8
SWA_WINDOW = 128

LANES = 128
SUBLANES = 8
VMEM_LIMIT_BYTES = 56 * 1024 * 1024

GLA_CHUNK = 64
NEG_BIG = -1e30
MOE_TILE = 256


def _cparams(sem):
    return pltpu.CompilerParams(dimension_semantics=sem, vmem_limit_bytes=VMEM_LIMIT_BYTES)


def _split3(x):
    hi = x.astype(BF16)
    r1 = x - hi.astype(F32)
    mid = r1.astype(BF16)
    lo = (r1 - mid.astype(F32)).astype(BF16)
    return hi, mid, lo


def _dot(a, b):
    return jnp.dot(a, b, preferred_element_type=F32)


def _dot_nt(a, b):
    return lax.dot_general(a, b, (((1,), (1,)), ((), ())), preferred_element_type=F32)


def _dot_tn(a, b):
    return lax.dot_general(a, b, (((0,), (0,)), ((), ())), preferred_element_type=F32)


def _rmsnorm_body(x_ref, g_ref, *o_refs):
    x = x_ref[...]
    y = x * lax.rsqrt(jnp.mean(x * x, axis=-1, keepdims=True) + EPS)
    for i, o_ref in enumerate(o_refs):
        o_ref[...] = (y * g_ref[i:i + 1, :]).astype(o_ref.dtype)


def rmsnorm(x, gains, out_dtype=BF16, tm=256):
    n, d = x.shape
    k = gains.shape[0]
    outs = pl.pallas_call(
        _rmsnorm_body,
        grid=(n // tm,),
        in_specs=[pl.BlockSpec((tm, d), lambda i: (i, 0)), pl.BlockSpec((k, d), lambda i: (0, 0))],
        out_specs=[pl.BlockSpec((tm, d), lambda i: (i, 0)) for _ in range(k)],
        out_shape=[jax.ShapeDtypeStruct((n, d), out_dtype) for _ in range(k)],
        compiler_params=_cparams(("parallel",)),
        name="rmsnorm",
    )(x, gains.astype(F32))
    return outs


def _mm_body(*refs, has_res, head_dim, scale, head_major):
    a_ref, w_ref = refs[0], refs[1]
    pos = 2
    res_ref = None
    if has_res:
        res_ref = refs[pos]
        pos += 1
    if head_dim:
        gain_ref, ind_ref, indt_ref = refs[pos:pos + 3]
        pos += 3
    o_ref = refs[pos]
    y = _dot(a_ref[...], w_ref[...])
    if head_dim:
        ssq = _dot((y * y).astype(BF16), ind_ref[...])
        inv = lax.rsqrt(ssq * (1.0 / head_dim) + EPS)
        hi, mid, _ = _split3(inv)
        it = indt_ref[...]
        y = y * (_dot(hi, it) + _dot(mid, it)) * (gain_ref[...] * scale)
    if has_res:
        y = y + res_ref[...]
    if head_major:
        hd = o_ref.shape[-1]
        for hh in range(o_ref.shape[0]):
            o_ref[hh] = y[:, hh * hd:(hh + 1) * hd].astype(o_ref.dtype)
    else:
        o_ref[...] = y.astype(o_ref.dtype)


def matmul(a, w, *, n_cols, col_block_off=0, tm, tn, res=None, out_dtype=F32,
           head_dim=0, head_gain=None, scale=1.0, head_major=0):
    n, k = a.shape
    assert n % tm == 0 and n_cols % tn == 0
    in_specs = [pl.BlockSpec((tm, k), lambda i, j: (i, 0)),
                pl.BlockSpec((k, tn), lambda i, j: (0, j + col_block_off))]
    args = [a, w]
    if res is not None:
        in_specs.append(pl.BlockSpec((tm, tn), lambda i, j: (i, j)))
        args.append(res)
    if head_dim:
        assert tn % head_dim == 0 and tn // head_dim <= LANES
        nh = tn // head_dim
        ind = np.zeros((tn, LANES), np.float32)
        ind[np.arange(tn), np.arange(tn) // head_dim] = 1.0
        gain_row = jnp.tile(head_gain.astype(F32), nh).reshape(1, tn)
        in_specs += [pl.BlockSpec((1, tn), lambda i, j: (0, 0)),
                     pl.BlockSpec((tn, LANES), lambda i, j: (0, 0)),
                     pl.BlockSpec((LANES, tn), lambda i, j: (0, 0))]
        args += [gain_row, jnp.asarray(ind, BF16), jnp.asarray(ind.T, BF16)]
    if head_major:
        assert tn % head_major == 0
        out_spec = pl.BlockSpec((tn // head_major, tm, head_major), lambda i, j: (j, i, 0))
        out_shape = jax.ShapeDtypeStruct((n_cols // head_major, n, head_major), out_dtype)
    else:
        out_spec = pl.BlockSpec((tm, tn), lambda i, j: (i, j))
        out_shape = jax.ShapeDtypeStruct((n, n_cols), out_dtype)
    return pl.pallas_call(
        functools.partial(_mm_body, has_res=res is not None, head_dim=head_dim, scale=scale,
                          head_major=head_major),
        grid=(n // tm, n_cols // tn),
        in_specs=in_specs,
        out_specs=out_spec,
        out_shape=out_shape,
        compiler_params=_cparams(("parallel", "arbitrary")),
        name="matmul",
    )(*args)


def _gla_tables(c):
    levels = []
    l = c // 2
    while l >= 1:
        levels.append(l)
        l //= 2
    rows = np.arange(c)
    wcum = (rows[None, :] <= rows[:, None]).astype(np.float32)
    wlev = []
    masks = [np.eye(c, dtype=np.float32)]
    for l in levels:
        blk = rows // l
        odd = (blk % 2) == 1
        ref = np.where(odd, blk * l, (blk + 1) * l)
        lo = np.minimum(rows, ref)
        hi = np.maximum(rows, ref)
        wlev.append(((rows[None, :] > lo[:, None]) & (rows[None, :] <= hi[:, None])).astype(np.float32))
        masks.append((odd[:, None] & (blk[None, :] == blk[:, None] - 1)).astype(np.float32))
    return levels, wcum, np.concatenate(wlev, axis=0), np.stack(masks, axis=0)


def _gla_body(q_ref, k_ref, v_ref, r_ref, z_ref, wup_ref, bd_ref, gout_ref, wcum_ref, wlev_ref, mask_ref,
              o_ref, st_ref, e_ref, x_ref, qe_ref, kr_ref, acc_ref, upd_ref, *, c, levels, scale, tau):
    tb, dk = q_ref.shape
    nch = tb // c
    nlev = len(levels)

    @pl.when(pl.program_id(2) == 0)
    def _():
        st_ref[...] = jnp.zeros_like(st_ref)

    zh, zm, _ = _split3(z_ref[...])
    wh, wm = wup_ref[0], wup_ref[1]
    pre = _dot(zh, wh) + _dot(zh, wm) + _dot(zm, wh) + bd_ref[...]
    g = (jnp.minimum(pre, 0.0) - jnp.log(1.0 + jnp.exp(-jnp.abs(pre)))) * (1.0 / tau)
    g_hi = g.astype(BF16)
    g_lo = (g - g_hi.astype(F32)).astype(BF16)
    for ci in range(nch):
        rows = slice(ci * c, (ci + 1) * c)
        ec = _dot(wcum_ref[...], jnp.concatenate([g_hi[rows], g_lo[rows]], axis=1))
        e_ref[0, rows, :] = ec[:, :dk] + ec[:, dk:]
        el = _dot(wlev_ref[...], g_hi[rows])
        for li in range(nlev):
            e_ref[li + 1, rows, :] = el[li * c:(li + 1) * c]
    q = q_ref[...].astype(F32) * scale
    k = k_ref[...].astype(F32)
    ecum = e_ref[0]
    qe_ref[...] = (q * jnp.exp(ecum)).astype(BF16)
    lasts = [ecum[(ci + 1) * c - 1:(ci + 1) * c, :] for ci in range(nch)]
    last_b = jnp.concatenate([jnp.broadcast_to(l, (c, dk)) for l in lasts], axis=0)
    kr_ref[...] = (k * jnp.exp(last_b - ecum)).astype(BF16)
    row = lax.broadcasted_iota(jnp.int32, (tb, dk), 0)
    for li, l in enumerate(levels):
        odd = ((row >> int(math.log2(l))) & 1) == 1
        x_ref[li] = (jnp.where(odd, q, k) * jnp.exp(e_ref[li + 1])).astype(BF16)
    qb = q.astype(BF16)
    kb = k.astype(BF16)
    for ci in range(nch):
        rows = slice(ci * c, (ci + 1) * c)
        a = mask_ref[0] * _dot_nt(qb[rows], kb[rows])
        for li in range(nlev):
            xc = x_ref[li, rows, :]
            a = a + mask_ref[li + 1] * _dot_nt(xc, xc)
        v = v_ref[rows, :]
        acc_ref[rows, :] = _dot(a.astype(BF16), v)
        upd_ref[ci] = _dot_tn(v, kr_ref[rows, :])
    st = st_ref[...]
    for ci in range(nch):
        rows = slice(ci * c, (ci + 1) * c)
        acc_ref[rows, :] = acc_ref[rows, :] + _dot_nt(qe_ref[rows, :], st.astype(BF16))
        st = st * jnp.exp(lasts[ci]) + upd_ref[ci]
    st_ref[...] = st
    o = acc_ref[...]
    o = o * lax.rsqrt(jnp.mean(o * o, axis=-1, keepdims=True) + EPS) * gout_ref[...]
    r = r_ref[...].astype(F32)
    o_ref[...] = (o * (r / (1.0 + jnp.exp(-r)))).astype(o_ref.dtype)


def gla_core(proj, z, w_up, b_decay, g_out, *, batch, heads, dk, dv, tb=512, c=GLA_CHUNK):
    n = proj.shape[0]
    t = n // batch
    nb = t // tb
    levels, wcum, wlev, masks = _gla_tables(c)
    nlev = len(levels)
    rank = w_up.shape[0]
    wup_pad = jnp.zeros((LANES, heads * dk), F32).at[:rank].set(w_up.astype(F32))
    wup2 = jnp.stack(_split3(wup_pad)[:2], axis=0)
    kv0 = 2 * heads * dk // dv
    row_map = lambda b, h, i: b * nb + i
    const2 = lambda b, h, i: (0, 0)
    return pl.pallas_call(
        functools.partial(_gla_body, c=c, levels=levels, scale=dk ** -0.5, tau=GLA_GATE_TAU),
        grid=(batch, heads, nb),
        in_specs=[
            pl.BlockSpec((tb, dk), lambda b, h, i: (row_map(b, h, i), h)),
            pl.BlockSpec((tb, dk), lambda b, h, i: (row_map(b, h, i), heads + h)),
            pl.BlockSpec((tb, dv), lambda b, h, i: (row_map(b, h, i), kv0 + h)),
            pl.BlockSpec((tb, dv), lambda b, h, i: (row_map(b, h, i), kv0 + heads + h)),
            pl.BlockSpec((tb, LANES), lambda b, h, i: (row_map(b, h, i), 0)),
            pl.BlockSpec((2, LANES, dk), lambda b, h, i: (0, 0, h)),
            pl.BlockSpec((1, dk), lambda b, h, i: (0, h)),
            pl.BlockSpec((1, dv), const2),
            pl.BlockSpec(wcum.shape, const2),
            pl.BlockSpec(wlev.shape, const2),
            pl.BlockSpec(masks.shape, lambda b, h, i: (0, 0, 0)),
        ],
        out_specs=pl.BlockSpec((tb, dv), lambda b, h, i: (row_map(b, h, i), h)),
        out_shape=jax.ShapeDtypeStruct((n, heads * dv), BF16),
        scratch_shapes=[pltpu.VMEM((dv, dk), F32),
                        pltpu.VMEM((nlev + 1, tb, dk), F32),
                        pltpu.VMEM((nlev, tb, dk), BF16),
                        pltpu.VMEM((tb, dk), BF16),
                        pltpu.VMEM((tb, dk), BF16),
                        pltpu.VMEM((tb, dv), F32),
                        pltpu.VMEM((tb // c, dv, dk), F32)],
        compiler_params=_cparams(("parallel", "parallel", "arbitrary")),
        name="gla_core",
    )(proj, proj, proj, proj, z, wup2, b_decay.reshape(1, -1).astype(F32), g_out.reshape(1, -1).astype(F32),
      jnp.asarray(wcum, BF16), jnp.asarray(wlev, BF16), jnp.asarray(masks, F32))


def _swa_body(q_ref, kc_ref, kp_ref, vc_ref, vp_ref, sink_ref, bias_ref, o_ref, *, group, blk):
    kvh, _, hd = kc_ref.shape
    gw = group * hd
    for h in range(kvh):
        qh = q_ref[:, h * gw:(h + 1) * gw]
        q_all = jnp.concatenate([qh[:, g * hd:(g + 1) * hd] for g in range(group)], axis=0)
        k2 = jnp.concatenate([kp_ref[h], kc_ref[h]], axis=0)
        v2 = jnp.concatenate([vp_ref[h], vc_ref[h]], axis=0)
        s = _dot_nt(k2, q_all) + bias_ref[...]
        sink = sink_ref[h]
        m = jnp.maximum(jnp.max(s, axis=0, keepdims=True), sink)
        p = jnp.exp(s - m)
        denom = jnp.sum(p, axis=0, keepdims=True) + jnp.exp(sink - m)
        o_t = _dot_tn(v2, p.astype(BF16)) * (1.0 / denom)
        o = jnp.concatenate([o_t[:, g * blk:(g + 1) * blk].T for g in range(group)], axis=1)
        o_ref[:, h * gw:(h + 1) * gw] = o.astype(o_ref.dtype)


def swa_attention(q, k, v, sinks, *, batch, blk=SWA_WINDOW):
    n, qw = q.shape
    kvh, _, hd = k.shape
    group = qw // hd // kvh
    nb = n // batch // blk
    ki = np.arange(2 * blk)[:, None]
    qi = np.tile(np.arange(blk), group)[None, :]
    dist = blk + qi - ki
    band = (dist >= 0) & (dist < blk)
    bias = np.stack([np.where(band & (ki >= blk), 0.0, NEG_BIG), np.where(band, 0.0, NEG_BIG)], axis=0)
    sink_rows = jnp.repeat(sinks.astype(F32).reshape(kvh, group), blk, axis=1).reshape(kvh, 1, group * blk)
    row = lambda b, j: b * nb + j
    prev = lambda b, j: (0, b * nb + jnp.maximum(j - 1, 0), 0)
    cur = lambda b, j: (0, row(b, j), 0)
    kv_c = pl.BlockSpec((kvh, blk, hd), cur)
    kv_p = pl.BlockSpec((kvh, blk, hd), prev)
    return pl.pallas_call(
        functools.partial(_swa_body, group=group, blk=blk),
        grid=(batch, nb),
        in_specs=[
            pl.BlockSpec((blk, qw), lambda b, j: (row(b, j), 0)),
            kv_c, kv_p, kv_c, kv_p,
            pl.BlockSpec((kvh, 1, group * blk), lambda b, j: (0, 0, 0)),
            pl.BlockSpec((None, 2 * blk, group * blk), lambda b, j: (jnp.minimum(j, 1), 0, 0)),
        ],
        out_specs=pl.BlockSpec((blk, qw), lambda b, j: (row(b, j), 0)),
        out_shape=jax.ShapeDtypeStruct((n, qw), BF16),
        compiler_params=_cparams(("parallel", "arbitrary")),
        name="swa_attention",
    )(q, k, k, v, v, sink_rows, jnp.asarray(bias.astype(np.float32)))


def _pack_rows(y, o_ref):
    rows, d = y.shape
    ns = _slab_rows(d)
    for s in range(ns):
        lo = y[:, (2 * s) * LANES:(2 * s + 1) * LANES]
        hi = y[:, (2 * s + 1) * LANES:(2 * s + 2) * LANES]
        lo_bits = lax.bitcast_convert_type(lo.astype(BF16).astype(F32), U32)
        hi_bits = lax.bitcast_convert_type(hi.astype(BF16).astype(F32), U32)
        o_ref[pl.ds(s, rows, stride=ns), :] = (lo_bits >> 16) | (hi_bits & jnp.uint32(0xFFFF0000))


def _slab_rows(d):
    assert d % (2 * SUBLANES * LANES) == 0
    return d // (2 * LANES)


def _unpack_rows(x_ref, s, ns):
    w = x_ref[pl.ds(s, x_ref.shape[0] // ns, stride=ns), :]
    lo = lax.bitcast_convert_type(w << 16, F32)
    hi = lax.bitcast_convert_type(w & jnp.uint32(0xFFFF0000), F32)
    return lo, hi


def _router_body(x_ref, g_ref, wr_ref, br_ref, tri_ref, xn_ref, meta_ref, wts_ref, cnt_ref, carry_ref,
                 *, groups, per_group):
    @pl.when(pl.program_id(0) == 0)
    def _():
        carry_ref[...] = jnp.zeros_like(carry_ref)

    x = x_ref[...]
    xn = x * lax.rsqrt(jnp.mean(x * x, axis=-1, keepdims=True) + EPS) * g_ref[...]
    _pack_rows(xn, xn_ref)
    xh, xm, _ = _split3(xn)
    wh, wm = wr_ref[0], wr_ref[1]
    logits = _dot(xh, wh) + _dot(xh, wm) + _dot(xm, wh) + br_ref[...]
    lane = lax.broadcasted_iota(jnp.int32, logits.shape, 1).astype(F32)
    big = jnp.float32(LANES)
    neg = jnp.float32(-jnp.inf)
    gl = jnp.where(lane < groups, logits, neg)
    gm = jnp.max(gl, axis=-1, keepdims=True)
    p_top = 1.0 / jnp.sum(jnp.exp(gl - gm), axis=-1, keepdims=True)
    g_idx = jnp.min(jnp.where(gl == gm, lane, big), axis=-1, keepdims=True)
    lo = groups + g_idx * per_group
    el = jnp.where((lane >= lo) & (lane < lo + per_group), logits, neg)
    v1 = jnp.max(el, axis=-1, keepdims=True)
    i1 = jnp.min(jnp.where(el == v1, lane, big), axis=-1, keepdims=True)
    el2 = jnp.where(lane == i1, neg, el)
    v2 = jnp.max(el2, axis=-1, keepdims=True)
    i2 = jnp.min(jnp.where(el2 == v2, lane, big), axis=-1, keepdims=True)
    e21 = jnp.exp(v2 - v1)
    w1 = p_top / (1.0 + e21)
    w2 = p_top * e21 / (1.0 + e21)
    sel = jnp.where((lane == i1) | (lane == i2), 1.0, 0.0)
    before = _dot(tri_ref[...], sel.astype(BF16)) + carry_ref[...]
    r1 = jnp.sum(jnp.where(lane == i1, before, 0.0), axis=-1, keepdims=True)
    r2 = jnp.sum(jnp.where(lane == i2, before, 0.0), axis=-1, keepdims=True)
    carry = carry_ref[...] + jnp.sum(sel, axis=0, keepdims=True)
    carry_ref[...] = carry
    cnt_ref[...] = jnp.broadcast_to(carry, cnt_ref.shape)
    meta = jnp.where(lane == 0, i1 - groups, jnp.where(lane == 1, i2 - groups,
                     jnp.where(lane == 2, r1, jnp.where(lane == 3, r2, 0.0))))
    meta_ref[...] = meta.astype(jnp.int32)
    wts_ref[...] = jnp.where(lane == 0, w1, jnp.where(lane == 1, w2, 0.0))


def router(h, gain, w_group, b_group, w_expert, b_expert, tm=256):
    n, d = h.shape
    ns = _slab_rows(d)
    groups = w_group.shape[1]
    experts = w_expert.shape[1]
    wr = jnp.zeros((d, LANES), F32).at[:, :groups].set(w_group).at[:, groups:groups + experts].set(w_expert)
    br = jnp.zeros((1, LANES), F32).at[0, :groups].set(b_group).at[0, groups:groups + experts].set(b_expert)
    wr2 = jnp.stack(_split3(wr)[:2], axis=0)
    tri = jnp.asarray(np.tril(np.ones((tm, tm), np.float32), -1), BF16)
    row = lambda i: (i, 0)
    xn, meta, wts, cnt = pl.pallas_call(
        functools.partial(_router_body, groups=groups, per_group=experts // groups),
        grid=(n // tm,),
        in_specs=[pl.BlockSpec((tm, d), row), pl.BlockSpec((1, d), lambda i: (0, 0)),
                  pl.BlockSpec((2, d, LANES), lambda i: (0, 0, 0)), pl.BlockSpec((1, LANES), lambda i: (0, 0)),
                  pl.BlockSpec((tm, tm), lambda i: (0, 0))],
        out_specs=[pl.BlockSpec((tm * ns, LANES), row),
                   pl.BlockSpec((tm, LANES), row), pl.BlockSpec((tm, LANES), row),
                   pl.BlockSpec((SUBLANES, LANES), lambda i: (0, 0))],
        out_shape=[jax.ShapeDtypeStruct((n * ns, LANES), U32),
                   jax.ShapeDtypeStruct((n, LANES), jnp.int32),
                   jax.ShapeDtypeStruct((n, LANES), F32),
                   jax.ShapeDtypeStruct((SUBLANES, LANES), F32)],
        scratch_shapes=[pltpu.VMEM((1, LANES), F32)],
        compiler_params=_cparams(("arbitrary",)),
        name="moe_router",
    )(h, gain.reshape(1, d).astype(F32), wr2, br, tri)
    counts = cnt[0, groups:groups + experts].astype(jnp.int32)
    return xn, meta[:, 0:2], meta[:, 2:4], wts, counts


def _row_copy(src_ref, src_row, dst_ref, dst_row, sem, ns):
    src = src_ref.at[pl.ds(pl.multiple_of(src_row * ns, ns), ns)]
    dst = dst_ref.at[pl.ds(pl.multiple_of(dst_row * ns, ns), ns)]
    return pltpu.make_async_copy(src, dst, sem)


def _pad_bits(tm):
    return [1 << b for b in range(int(math.log2(tm)))]


def _dispatch_body(pad_start_ref, pad_count_ref, nu_ref, pos_ref, x_ref, o_hbm, zero_ref, sem, *, tm, tile,
                   n_experts, ns):
    t = pl.program_id(0)
    half = tile // 2

    @pl.when(t == 0)
    def _():
        zero_ref[...] = jnp.zeros_like(zero_ref)

        def zero_tile(i, c):
            cps = [pltpu.make_async_copy(
                zero_ref, o_hbm.at[pl.ds(pl.multiple_of((i * tile + k * half) * ns, ns), half * ns)], sem)
                for k in range(2)]
            for cp in cps:
                cp.start()
            for cp in cps:
                cp.wait()
            return c
        lax.fori_loop(nu_ref[0], o_hbm.shape[0] // (tile * ns), zero_tile, 0)
        for phase in ("start", "wait"):
            for e in range(n_experts):
                cnt = pad_count_ref[e]
                off = pad_start_ref[e]
                for bit in _pad_bits(tile):
                    dst0 = pl.multiple_of(off * ns, ns)
                    cp = pltpu.make_async_copy(zero_ref.at[pl.ds(0, bit * ns)],
                                               o_hbm.at[pl.ds(dst0, bit * ns)], sem)

                    @pl.when((cnt & bit) != 0)
                    def _():
                        cp.start() if phase == "start" else cp.wait()
                    off = off + (cnt & bit)

    def start(r, c):
        _row_copy(x_ref, r, o_hbm, pos_ref[0, 0, r], sem, ns).start(priority=0)
        _row_copy(x_ref, r, o_hbm, pos_ref[0, 0, tm + r], sem, ns).start(priority=1)
        return c
    lax.fori_loop(0, tm, start, 0)

    def wait(r, c):
        _row_copy(x_ref, 0, o_hbm, 0, sem, ns).wait()
        _row_copy(x_ref, 0, o_hbm, 0, sem, ns).wait()
        return c
    lax.fori_loop(0, tm, wait, 0)


def moe_dispatch(xn, pos, pad_start, pad_count, n_used, p_rows, tile, ns, tm=256):
    n = xn.shape[0] // ns
    nt = n // tm
    n_experts = pad_start.shape[0]
    pos_t = pos.reshape(nt, tm, 2).transpose(0, 2, 1).reshape(nt, 1, 2 * tm)
    return pl.pallas_call(
        functools.partial(_dispatch_body, tm=tm, tile=tile, n_experts=n_experts, ns=ns),
        grid_spec=pltpu.PrefetchScalarGridSpec(
            num_scalar_prefetch=3,
            grid=(nt,),
            in_specs=[pl.BlockSpec((1, 1, 2 * tm), lambda t, ps, pc, nu: (t, 0, 0), memory_space=pltpu.SMEM),
                      pl.BlockSpec((tm * ns, LANES), lambda t, ps, pc, nu: (t, 0))],
            out_specs=pl.BlockSpec(memory_space=pl.ANY),
            scratch_shapes=[pltpu.VMEM((tile // 2 * ns, LANES), xn.dtype), pltpu.SemaphoreType.DMA(())],
        ),
        out_shape=jax.ShapeDtypeStruct((p_rows * ns, LANES), xn.dtype),
        compiler_params=_cparams(("arbitrary",)),
        name="moe_dispatch",
    )(pad_start, pad_count, n_used, pos_t, xn)


def _moe_up_body(te_ref, nu_ref, x_ref, wg_ref, wu_ref, o_ref, wgb_ref, wub_ref):
    t = pl.program_id(1)
    changed = te_ref[t] != te_ref[jnp.maximum(t - 1, 0)]

    @pl.when((t < nu_ref[0]) & ((t == 0) | changed))
    def _():
        wgb_ref[...] = wg_ref[...].astype(BF16)
        wub_ref[...] = wu_ref[...].astype(BF16)

    @pl.when(t < nu_ref[0])
    def _():
        ns = _slab_rows(wgb_ref.shape[0])
        x = jnp.concatenate([xs.astype(BF16) for s in range(ns) for xs in _unpack_rows(x_ref, s, ns)], axis=1)
        a = _dot(x, wgb_ref[...])
        u = _dot(x, wub_ref[...])
        o_ref[...] = (a / (1.0 + jnp.exp(-a)) * u).astype(o_ref.dtype)

    @pl.when(t >= nu_ref[0])
    def _():
        o_ref[...] = jnp.zeros_like(o_ref)


def moe_up(xs, tile_expert, n_used, w_gate_up, layer, tm, fc):
    d, f2 = w_gate_up.shape[2:]
    ns = _slab_rows(d)
    p = xs.shape[0] // ns
    f = f2 // 2
    nc = f // fc
    tile = lambda t, nu: jnp.minimum(t, nu[0] - 1)
    return pl.pallas_call(
        _moe_up_body,
        grid_spec=pltpu.PrefetchScalarGridSpec(
            num_scalar_prefetch=2,
            grid=(nc, p // tm),
            in_specs=[pl.BlockSpec((tm * ns, LANES), lambda c, t, te, nu: (tile(t, nu), 0)),
                      pl.BlockSpec((None, None, d, fc), lambda c, t, te, nu: (layer, te[tile(t, nu)], 0, c)),
                      pl.BlockSpec((None, None, d, fc), lambda c, t, te, nu: (layer, te[tile(t, nu)], 0, nc + c))],
            out_specs=pl.BlockSpec((tm, fc), lambda c, t, te, nu: (t, c)),
            scratch_shapes=[pltpu.VMEM((d, fc), BF16), pltpu.VMEM((d, fc), BF16)],
        ),
        out_shape=jax.ShapeDtypeStruct((p, f), BF16),
        compiler_params=_cparams(("arbitrary", "arbitrary")),
        name="moe_up",
    )(tile_expert, n_used, xs, w_gate_up, w_gate_up)


def _moe_down_body(te_ref, nu_ref, h_ref, w_ref, o_ref, wb_ref):
    t = pl.program_id(0)
    changed = te_ref[t] != te_ref[jnp.maximum(t - 1, 0)]

    @pl.when((t < nu_ref[0]) & ((t == 0) | changed))
    def _():
        wb_ref[...] = w_ref[...].astype(BF16)

    @pl.when(t < nu_ref[0])
    def _():
        _pack_rows(_dot(h_ref[...], wb_ref[...]), o_ref)

    @pl.when(t >= nu_ref[0])
    def _():
        o_ref[...] = jnp.zeros_like(o_ref)


def moe_down(hs, tile_expert, n_used, w_down, layer, tm):
    p, f = hs.shape
    d = w_down.shape[3]
    tile = lambda t, nu: jnp.minimum(t, nu[0] - 1)
    return pl.pallas_call(
        _moe_down_body,
        grid_spec=pltpu.PrefetchScalarGridSpec(
            num_scalar_prefetch=2,
            grid=(p // tm,),
            in_specs=[pl.BlockSpec((tm, f), lambda t, te, nu: (tile(t, nu), 0)),
                      pl.BlockSpec((None, None, f, d), lambda t, te, nu: (layer, te[tile(t, nu)], 0, 0))],
            out_specs=pl.BlockSpec((tm * _slab_rows(d), LANES), lambda t, te, nu: (t, 0)),
            scratch_shapes=[pltpu.VMEM((f, d), BF16)],
        ),
        out_shape=jax.ShapeDtypeStruct((p * _slab_rows(d), LANES), U32),
        compiler_params=_cparams(("arbitrary",)),
        name="moe_down",
    )(tile_expert, n_used, hs, w_down)


def _combine_body(pos_ref, pos_next_ref, h_ref, w_ref, y_hbm, o_ref, buf_ref, sems, *, tm):
    ns = _slab_rows(h_ref.shape[1])
    t = pl.program_id(0)
    nt = pl.num_programs(0)

    def fetch(idx_ref, slot):
        def start(r, c):
            _row_copy(y_hbm, idx_ref[0, 0, r], buf_ref.at[slot, 0], r, sems.at[slot], ns).start(priority=0)
            _row_copy(y_hbm, idx_ref[0, 0, tm + r], buf_ref.at[slot, 1], r, sems.at[slot], ns).start(priority=1)
            return c
        lax.fori_loop(0, tm, start, 0)

    def step(slot):
        @pl.when(t == 0)
        def _():
            fetch(pos_ref, slot)

        @pl.when(t + 1 < nt)
        def _():
            fetch(pos_next_ref, 1 - slot)

        def wait(r, c):
            _row_copy(y_hbm, 0, buf_ref.at[slot, 0], r, sems.at[slot], ns).wait()
            _row_copy(y_hbm, 0, buf_ref.at[slot, 1], r, sems.at[slot], ns).wait()
            return c
        lax.fori_loop(0, tm, wait, 0)
        w = w_ref[...]
        w0 = jnp.broadcast_to(w[:, 0:1], (tm, LANES))
        w1 = jnp.broadcast_to(w[:, 1:2], (tm, LANES))
        for s in range(ns):
            a_lo, a_hi = _unpack_rows(buf_ref.at[slot, 0], s, ns)
            b_lo, b_hi = _unpack_rows(buf_ref.at[slot, 1], s, ns)
            c_lo = slice((2 * s) * LANES, (2 * s + 1) * LANES)
            c_hi = slice((2 * s + 1) * LANES, (2 * s + 2) * LANES)
            o_ref[:, c_lo] = h_ref[:, c_lo] + w0 * a_lo + w1 * b_lo
            o_ref[:, c_hi] = h_ref[:, c_hi] + w0 * a_hi + w1 * b_hi

    for slot in range(2):
        pl.when(t % 2 == slot)(functools.partial(step, slot))


def moe_combine(h, wts, ys, pos, tm=256):
    n, d = h.shape
    nt = n // tm
    pos_t = pos.reshape(nt, tm, 2).transpose(0, 2, 1).reshape(nt, 1, 2 * tm)
    return pl.pallas_call(
        functools.partial(_combine_body, tm=tm),
        grid=(nt,),
        in_specs=[pl.BlockSpec((1, 1, 2 * tm), lambda t: (t, 0, 0), memory_space=pltpu.SMEM),
                  pl.BlockSpec((1, 1, 2 * tm), lambda t: (jnp.minimum(t + 1, nt - 1), 0, 0),
                               memory_space=pltpu.SMEM),
                  pl.BlockSpec((tm, d), lambda t: (t, 0)),
                  pl.BlockSpec((tm, LANES), lambda t: (t, 0)),
                  pl.BlockSpec(memory_space=pl.ANY)],
        out_specs=pl.BlockSpec((tm, d), lambda t: (t, 0)),
        out_shape=jax.ShapeDtypeStruct((n, d), F32),
        scratch_shapes=[pltpu.VMEM((2, 2, tm * _slab_rows(d), LANES), ys.dtype),
                        pltpu.SemaphoreType.DMA((2,))],
        compiler_params=_cparams(("arbitrary",)),
        name="moe_combine",
    )(pos_t, pos_t, h, wts, ys)


def _routing_tables(ids, ranks, counts, tm):
    n = ids.shape[0]
    n_experts = counts.shape[0]
    padded = ((counts + tm - 1) // tm) * tm
    ends = jnp.cumsum(padded)
    offs = ends - padded
    experts = jnp.arange(n_experts, dtype=jnp.int32)
    pos = ranks + jnp.sum(jnp.where(ids[:, :, None] == experts[None, None, :], offs[None, None, :], 0), axis=-1)
    p_rows = ((2 * n + n_experts * (tm - 1)) // tm) * tm
    tile_start = jnp.arange(p_rows // tm, dtype=jnp.int32) * tm
    tile_expert = jnp.minimum(jnp.sum((ends[None, :] <= tile_start[:, None]).astype(jnp.int32), axis=1),
                              n_experts - 1)
    n_used = (ends[-1:] // tm).astype(jnp.int32)
    return pos.astype(jnp.int32), tile_expert, n_used, (offs + counts).astype(jnp.int32), \
        (padded - counts).astype(jnp.int32), p_rows


def hier_moe_block(h, gain, w_group, b_group, w_expert, b_expert, w_gate_up, w_down, layer, *,
                   tm=MOE_TILE, fc=256):
    xn, ids, ranks, wts, counts = router(h, gain, w_group, b_group, w_expert, b_expert)
    pos, tile_expert, n_used, pad_start, pad_count, p_rows = _routing_tables(ids, ranks, counts, tm)
    xs = moe_dispatch(xn, pos, pad_start, pad_count, n_used, p_rows, tm, _slab_rows(h.shape[1]))
    hs = moe_up(xs, tile_expert, n_used, w_gate_up, layer, tm, fc)
    ys = moe_down(hs, tile_expert, n_used, w_down, layer, tm)
    return moe_combine(h, wts, ys, pos)


def kernel(x, a_norm_g, a_w_in, a_w_decay_up, a_b_decay, a_out_norm_g, a_w_out, kv_norm_g, w_kv, k_norm_g,
           b_norm_g, b_w_q, b_q_norm_g, b_sinks, b_w_out, ffn_norm_g, moe_w_group, moe_b_group, moe_w_expert,
           moe_b_expert, moe_w_gate_up, moe_w_down):
    bsz, t, d = x.shape
    n = bsz * t
    h = x.reshape(n, d)
    tm_mm = 1024

    def moe(h, layer):
        return hier_moe_block(h, ffn_norm_g[layer], moe_w_group[layer], moe_b_group[layer],
                              moe_w_expert[layer], moe_b_expert[layer], moe_w_gate_up, moe_w_down, layer)

    heads = GLA_HEADS
    dk_all = a_w_decay_up.shape[2]
    dk = dk_all // heads
    dv_all = a_w_out.shape[1]
    dv = dv_all // heads
    n_main = 2 * dk_all + 2 * dv_all
    (xn,) = rmsnorm(h, a_norm_g[0:1])
    w_in = a_w_in[0].astype(BF16)
    proj = matmul(xn, w_in, n_cols=n_main, tm=tm_mm, tn=1024, out_dtype=BF16)
    w_z = jnp.zeros((d, LANES), BF16).at[:, :GLA_GATE_RANK].set(a_w_in[0][:, n_main:].astype(BF16))
    z = matmul(xn, w_z, n_cols=LANES, tm=tm_mm, tn=LANES, out_dtype=F32)
    gated = gla_core(proj, z, a_w_decay_up[0], a_b_decay[0], a_out_norm_g[0],
                     batch=bsz, heads=heads, dk=dk, dv=dv)
    h = matmul(gated, a_w_out[0].astype(BF16), n_cols=d, tm=tm_mm, tn=512, res=h)
    h = moe(h, 0)

    hd = SWA_HEAD_DIM
    kvh = SWA_KV_HEADS
    hq = b_w_q.shape[2] // hd
    xn_kv, xn_b = rmsnorm(h, jnp.stack([kv_norm_g, b_norm_g[0]], axis=0))
    w_kv_b = w_kv.astype(BF16)
    kw = kvh * hd
    k_sh = matmul(xn_kv, w_kv_b, n_cols=kw, tm=tm_mm, tn=kw, out_dtype=BF16, head_dim=hd, head_gain=k_norm_g,
                  head_major=hd)
    v_sh = matmul(xn_kv, w_kv_b, n_cols=kw, col_block_off=1, tm=tm_mm, tn=kw, out_dtype=BF16, head_major=hd)
    q = matmul(xn_b, b_w_q[0].astype(BF16), n_cols=hq * hd, tm=tm_mm, tn=1024, out_dtype=BF16,
               head_dim=hd, head_gain=b_q_norm_g[0], scale=hd ** -0.5)
    o = swa_attention(q, k_sh, v_sh, b_sinks[0], batch=bsz)
    h = matmul(o, b_w_out[0].astype(BF16), n_cols=d, tm=tm_mm, tn=512, res=h)
    h = moe(h, 1)
    return h.reshape(bsz, t, d)
```

```python
import functools
import math

import numpy as np
import jax
import jax.numpy as jnp
from jax import lax
from jax.experimental import pallas as pl
from jax.experimental.pallas import tpu as pltpu

F32 = jnp.float32
BF16 = jnp.bfloat16
U32 = jnp.uint32
EPS = 1e-6

GLA_HEADS = 8
GLA_GATE_RANK = 16
GLA_GATE_TAU = 16.0
SWA_HEAD_DIM = 64
SWA_KV_HEADS = 8
SWA_WINDOW = 128

LANES = 128
SUBLANES = 8
VMEM_LIMIT_BYTES = 56 * 1024 * 1024

GLA_CHUNK = 64
NEG_BIG = -1e30
MOE_TILE = 256


def _cparams(sem):
    return pltpu.CompilerParams(dimension_semantics=sem, vmem_limit_bytes=VMEM_LIMIT_BYTES)


def _split3(x):
    hi = x.astype(BF16)
    r1 = x - hi.astype(F32)
    mid = r1.astype(BF16)
    lo = (r1 - mid.astype(F32)).astype(BF16)
    return hi, mid, lo


def _dot(a, b):
    return jnp.dot(a, b, preferred_element_type=F32)


def _dot_nt(a, b):
    return lax.dot_general(a, b, (((1,), (1,)), ((), ())), preferred_element_type=F32)


def _dot_tn(a, b):
    return lax.dot_general(a, b, (((0,), (0,)), ((), ())), preferred_element_type=F32)


def _rmsnorm_body(x_ref, g_ref, *o_refs):
    x = x_ref[...]
    y = x * lax.rsqrt(jnp.mean(x * x, axis=-1, keepdims=True) + EPS)
    for i, o_ref in enumerate(o_refs):
        o_ref[...] = (y * g_ref[i:i + 1, :]).astype(o_ref.dtype)


def rmsnorm(x, gains, out_dtype=BF16, tm=256):
    n, d = x.shape
    k = gains.shape[0]
    outs = pl.pallas_call(
        _rmsnorm_body,
        grid=(n // tm,),
        in_specs=[pl.BlockSpec((tm, d), lambda i: (i, 0)), pl.BlockSpec((k, d), lambda i: (0, 0))],
        out_specs=[pl.BlockSpec((tm, d), lambda i: (i, 0)) for _ in range(k)],
        out_shape=[jax.ShapeDtypeStruct((n, d), out_dtype) for _ in range(k)],
        compiler_params=_cparams(("parallel",)),
        name="rmsnorm",
    )(x, gains.astype(F32))
    return outs


def _mm_body(*refs, has_res, head_dim, scale, head_major):
    a_ref, w_ref = refs[0], refs[1]
    pos = 2
    res_ref = None
    if has_res:
        res_ref = refs[pos]
        pos += 1
    if head_dim:
        gain_ref, ind_ref, indt_ref = refs[pos:pos + 3]
        pos += 3
    o_ref = refs[pos]
    y = _dot(a_ref[...], w_ref[...])
    if head_dim:
        ssq = _dot((y * y).astype(BF16), ind_ref[...])
        inv = lax.rsqrt(ssq * (1.0 / head_dim) + EPS)
        hi, mid, _ = _split3(inv)
        it = indt_ref[...]
        y = y * (_dot(hi, it) + _dot(mid, it)) * (gain_ref[...] * scale)
    if has_res:
        y = y + res_ref[...]
    if head_major:
        hd = o_ref.shape[-1]
        for hh in range(o_ref.shape[0]):
            o_ref[hh] = y[:, hh * hd:(hh + 1) * hd].astype(o_ref.dtype)
    else:
        o_ref[...] = y.astype(o_ref.dtype)


def matmul(a, w, *, n_cols, col_block_off=0, tm, tn, res=None, out_dtype=F32,
           head_dim=0, head_gain=None, scale=1.0, head_major=0):
    n, k = a.shape
    assert n % tm == 0 and n_cols % tn == 0
    in_specs = [pl.BlockSpec((tm, k), lambda i, j: (i, 0)),
                pl.BlockSpec((k, tn), lambda i, j: (0, j + col_block_off))]
    args = [a, w]
    if res is not None:
        in_specs.append(pl.BlockSpec((tm, tn), lambda i, j: (i, j)))
        args.append(res)
    if head_dim:
        assert tn % head_dim == 0 and tn // head_dim <= LANES
        nh = tn // head_dim
        ind = np.zeros((tn, LANES), np.float32)
        ind[np.arange(tn), np.arange(tn) // head_dim] = 1.0
        gain_row = jnp.tile(head_gain.astype(F32), nh).reshape(1, tn)
        in_specs += [pl.BlockSpec((1, tn), lambda i, j: (0, 0)),
                     pl.BlockSpec((tn, LANES), lambda i, j: (0, 0)),
                     pl.BlockSpec((LANES, tn), lambda i, j: (0, 0))]
        args += [gain_row, jnp.asarray(ind, BF16), jnp.asarray(ind.T, BF16)]
    if head_major:
        assert tn % head_major == 0
        out_spec = pl.BlockSpec((tn // head_major, tm, head_major), lambda i, j: (j, i, 0))
        out_shape = jax.ShapeDtypeStruct((n_cols // head_major, n, head_major), out_dtype)
    else:
        out_spec = pl.BlockSpec((tm, tn), lambda i, j: (i, j))
        out_shape = jax.ShapeDtypeStruct((n, n_cols), out_dtype)
    return pl.pallas_call(
        functools.partial(_mm_body, has_res=res is not None, head_dim=head_dim, scale=scale,
                          head_major=head_major),
        grid=(n // tm, n_cols // tn),
        in_specs=in_specs,
        out_specs=out_spec,
        out_shape=out_shape,
        compiler_params=_cparams(("parallel", "arbitrary")),
        name="matmul",
    )(*args)


def _gla_tables(c):
    levels = []
    l = c // 2
    while l >= 1:
        levels.append(l)
        l //= 2
    rows = np.arange(c)
    wcum = (rows[None, :] <= rows[:, None]).astype(np.float32)
    wlev = []
    masks = [np.eye(c, dtype=np.float32)]
    for l in levels:
        blk = rows // l
        odd = (blk % 2) == 1
        ref = np.where(odd, blk * l, (blk + 1) * l)
        lo = np.minimum(rows, ref)
        hi = np.maximum(rows, ref)
        wlev.append(((rows[None, :] > lo[:, None]) & (rows[None, :] <= hi[:, None])).astype(np.float32))
        masks.append((odd[:, None] & (blk[None, :] == blk[:, None] - 1)).astype(np.float32))
    return levels, wcum, np.concatenate(wlev, axis=0), np.stack(masks, axis=0)


def _gla_body(q_ref, k_ref, v_ref, r_ref, z_ref, wup_ref, bd_ref, gout_ref, wcum_ref, wlev_ref, mask_ref,
              o_ref, st_ref, e_ref, x_ref, qe_ref, kr_ref, acc_ref, upd_ref, *, c, levels, scale, tau):
    tb, dk = q_ref.shape
    nch = tb // c
    nlev = len(levels)

    @pl.when(pl.program_id(2) == 0)
    def _():
        st_ref[...] = jnp.zeros_like(st_ref)

    zh, zm, _ = _split3(z_ref[...])
    wh, wm = wup_ref[0], wup_ref[1]
    pre = _dot(zh, wh) + _dot(zh, wm) + _dot(zm, wh) + bd_ref[...]
    g = (jnp.minimum(pre, 0.0) - jnp.log(1.0 + jnp.exp(-jnp.abs(pre)))) * (1.0 / tau)
    g_hi = g.astype(BF16)
    g_lo = (g - g_hi.astype(F32)).astype(BF16)
    for ci in range(nch):
        rows = slice(ci * c, (ci + 1) * c)
        ec = _dot(wcum_ref[...], jnp.concatenate([g_hi[rows], g_lo[rows]], axis=1))
        e_ref[0, rows, :] = ec[:, :dk] + ec[:, dk:]
        el = _dot(wlev_ref[...], g_hi[rows])
        for li in range(nlev):
            e_ref[li + 1, rows, :] = el[li * c:(li + 1) * c]
    q = q_ref[...].astype(F32) * scale
    k = k_ref[...].astype(F32)
    ecum = e_ref[0]
    qe_ref[...] = (q * jnp.exp(ecum)).astype(BF16)
    lasts = [ecum[(ci + 1) * c - 1:(ci + 1) * c, :] for ci in range(nch)]
    last_b = jnp.concatenate([jnp.broadcast_to(l, (c, dk)) for l in lasts], axis=0)
    kr_ref[...] = (k * jnp.exp(last_b - ecum)).astype(BF16)
    row = lax.broadcasted_iota(jnp.int32, (tb, dk), 0)
    for li, l in enumerate(levels):
        odd = ((row >> int(math.log2(l))) & 1) == 1
        x_ref[li] = (jnp.where(odd, q, k) * jnp.exp(e_ref[li + 1])).astype(BF16)
    qb = q.astype(BF16)
    kb = k.astype(BF16)
    for ci in range(nch):
        rows = slice(ci * c, (ci + 1) * c)
        a = mask_ref[0] * _dot_nt(qb[rows], kb[rows])
        for li in range(nlev):
            xc = x_ref[li, rows, :]
            a = a + mask_ref[li + 1] * _dot_nt(xc, xc)
        v = v_ref[rows, :]
        acc_ref[rows, :] = _dot(a.astype(BF16), v)
        upd_ref[ci] = _dot_tn(v, kr_ref[rows, :])
    st = st_ref[...]
    for ci in range(nch):
        rows = slice(ci * c, (ci + 1) * c)
        acc_ref[rows, :] = acc_ref[rows, :] + _dot_nt(qe_ref[rows, :], st.astype(BF16))
        st = st * jnp.exp(lasts[ci]) + upd_ref[ci]
    st_ref[...] = st
    o = acc_ref[...]
    o = o * lax.rsqrt(jnp.mean(o * o, axis=-1, keepdims=True) + EPS) * gout_ref[...]
    r = r_ref[...].astype(F32)
    o_ref[...] = (o * (r / (1.0 + jnp.exp(-r)))).astype(o_ref.dtype)


def gla_core(proj, z, w_up, b_decay, g_out, *, batch, heads, dk, dv, tb=512, c=GLA_CHUNK):
    n = proj.shape[0]
    t = n // batch
    nb = t // tb
    levels, wcum, wlev, masks = _gla_tables(c)
    nlev = len(levels)
    rank = w_up.shape[0]
    wup_pad = jnp.zeros((LANES, heads * dk), F32).at[:rank].set(w_up.astype(F32))
    wup2 = jnp.stack(_split3(wup_pad)[:2], axis=0)
    kv0 = 2 * heads * dk // dv
    row_map = lambda b, h, i: b * nb + i
    const2 = lambda b, h, i: (0, 0)
    return pl.pallas_call(
        functools.partial(_gla_body, c=c, levels=levels, scale=dk ** -0.5, tau=GLA_GATE_TAU),
        grid=(batch, heads, nb),
        in_specs=[
            pl.BlockSpec((tb, dk), lambda b, h, i: (row_map(b, h, i), h)),
            pl.BlockSpec((tb, dk), lambda b, h, i: (row_map(b, h, i), heads + h)),
            pl.BlockSpec((tb, dv), lambda b, h, i: (row_map(b, h, i), kv0 + h)),
            pl.BlockSpec((tb, dv), lambda b, h, i: (row_map(b, h, i), kv0 + heads + h)),
            pl.BlockSpec((tb, LANES), lambda b, h, i: (row_map(b, h, i), 0)),
            pl.BlockSpec((2, LANES, dk), lambda b, h, i: (0, 0, h)),
            pl.BlockSpec((1, dk), lambda b, h, i: (0, h)),
            pl.BlockSpec((1, dv), const2),
            pl.BlockSpec(wcum.shape, const2),
            pl.BlockSpec(wlev.shape, const2),
            pl.BlockSpec(masks.shape, lambda b, h, i: (0, 0, 0)),
        ],
        out_specs=pl.BlockSpec((tb, dv), lambda b, h, i: (row_map(b, h, i), h)),
        out_shape=jax.ShapeDtypeStruct((n, heads * dv), BF16),
        scratch_shapes=[pltpu.VMEM((dv, dk), F32),
                        pltpu.VMEM((nlev + 1, tb, dk), F32),
                        pltpu.VMEM((nlev, tb, dk), BF16),
                        pltpu.VMEM((tb, dk), BF16),
                        pltpu.VMEM((tb, dk), BF16),
                        pltpu.VMEM((tb, dv), F32),
                        pltpu.VMEM((tb // c, dv, dk), F32)],
        compiler_params=_cparams(("parallel", "parallel", "arbitrary")),
        name="gla_core",
    )(proj, proj, proj, proj, z, wup2, b_decay.reshape(1, -1).astype(F32), g_out.reshape(1, -1).astype(F32),
      jnp.asarray(wcum, BF16), jnp.asarray(wlev, BF16), jnp.asarray(masks, F32))


def _swa_body(q_ref, kc_ref, kp_ref, vc_ref, vp_ref, sink_ref, bias_ref, o_ref, *, group, blk):
    kvh, _, hd = kc_ref.shape
    gw = group * hd
    for h in range(kvh):
        qh = q_ref[:, h * gw:(h + 1) * gw]
        q_all = jnp.concatenate([qh[:, g * hd:(g + 1) * hd] for g in range(group)], axis=0)
        k2 = jnp.concatenate([kp_ref[h], kc_ref[h]], axis=0)
        v2 = jnp.concatenate([vp_ref[h], vc_ref[h]], axis=0)
        s = _dot_nt(k2, q_all) + bias_ref[...]
        sink = sink_ref[h]
        m = jnp.maximum(jnp.max(s, axis=0, keepdims=True), sink)
        p = jnp.exp(s - m)
        denom = jnp.sum(p, axis=0, keepdims=True) + jnp.exp(sink - m)
        o_t = _dot_tn(v2, p.astype(BF16)) * (1.0 / denom)
        o = jnp.concatenate([o_t[:, g * blk:(g + 1) * blk].T for g in range(group)], axis=1)
        o_ref[:, h * gw:(h + 1) * gw] = o.astype(o_ref.dtype)


def swa_attention(q, k, v, sinks, *, batch, blk=SWA_WINDOW):
    n, qw = q.shape
    kvh, _, hd = k.shape
    group = qw // hd // kvh
    nb = n // batch // blk
    ki = np.arange(2 * blk)[:, None]
    qi = np.tile(np.arange(blk), group)[None, :]
    dist = blk + qi - ki
    band = (dist >= 0) & (dist < blk)
    bias = np.stack([np.where(band & (ki >= blk), 0.0, NEG_BIG), np.where(band, 0.0, NEG_BIG)], axis=0)
    sink_rows = jnp.repeat(sinks.astype(F32).reshape(kvh, group), blk, axis=1).reshape(kvh, 1, group * blk)
    row = lambda b, j: b * nb + j
    prev = lambda b, j: (0, b * nb + jnp.maximum(j - 1, 0), 0)
    cur = lambda b, j: (0, row(b, j), 0)
    kv_c = pl.BlockSpec((kvh, blk, hd), cur)
    kv_p = pl.BlockSpec((kvh, blk, hd), prev)
    return pl.pallas_call(
        functools.partial(_swa_body, group=group, blk=blk),
        grid=(batch, nb),
        in_specs=[
            pl.BlockSpec((blk, qw), lambda b, j: (row(b, j), 0)),
            kv_c, kv_p, kv_c, kv_p,
            pl.BlockSpec((kvh, 1, group * blk), lambda b, j: (0, 0, 0)),
            pl.BlockSpec((None, 2 * blk, group * blk), lambda b, j: (jnp.minimum(j, 1), 0, 0)),
        ],
        out_specs=pl.BlockSpec((blk, qw), lambda b, j: (row(b, j), 0)),
        out_shape=jax.ShapeDtypeStruct((n, qw), BF16),
        compiler_params=_cparams(("parallel", "arbitrary")),
        name="swa_attention",
    )(q, k, k, v, v, sink_rows, jnp.asarray(bias.astype(np.float32)))


def _pack_rows(y, o_ref):
    rows, d = y.shape
    ns = _slab_rows(d)
    for s in range(ns):
        lo = y[:, (2 * s) * LANES:(2 * s + 1) * LANES]
        hi = y[:, (2 * s + 1) * LANES:(2 * s + 2) * LANES]
        lo_bits = lax.bitcast_convert_type(lo.astype(BF16).astype(F32), U32)
        hi_bits = lax.bitcast_convert_type(hi.astype(BF16).astype(F32), U32)
        o_ref[pl.ds(s, rows, stride=ns), :] = (lo_bits >> 16) | (hi_bits & jnp.uint32(0xFFFF0000))


def _slab_rows(d):
    assert d % (2 * SUBLANES * LANES) == 0
    return d // (2 * LANES)


def _unpack_words(w):
    lo = lax.bitcast_convert_type(w << 16, F32)
    hi = lax.bitcast_convert_type(w & jnp.uint32(0xFFFF0000), F32)
    return lo, hi


def _unpack_rows(x_ref, s, ns):
    return _unpack_words(x_ref[pl.ds(s, x_ref.shape[0] // ns, stride=ns), :])


def _router_body(x_ref, g_ref, wr_ref, br_ref, tri_ref, xn_ref, meta_ref, wts_ref, cnt_ref, carry_ref,
                 *, groups, per_group):
    @pl.when(pl.program_id(0) == 0)
    def _():
        carry_ref[...] = jnp.zeros_like(carry_ref)

    x = x_ref[...]
    xn = x * lax.rsqrt(jnp.mean(x * x, axis=-1, keepdims=True) + EPS) * g_ref[...]
    _pack_rows(xn, xn_ref)
    xh, xm, _ = _split3(xn)
    wh, wm = wr_ref[0], wr_ref[1]
    logits = _dot(xh, wh) + _dot(xh, wm) + _dot(xm, wh) + br_ref[...]
    lane = lax.broadcasted_iota(jnp.int32, logits.shape, 1).astype(F32)
    big = jnp.float32(LANES)
    neg = jnp.float32(-jnp.inf)
    gl = jnp.where(lane < groups, logits, neg)
    gm = jnp.max(gl, axis=-1, keepdims=True)
    p_top = 1.0 / jnp.sum(jnp.exp(gl - gm), axis=-1, keepdims=True)
    g_idx = jnp.min(jnp.where(gl == gm, lane, big), axis=-1, keepdims=True)
    lo = groups + g_idx * per_group
    el = jnp.where((lane >= lo) & (lane < lo + per_group), logits, neg)
    v1 = jnp.max(el, axis=-1, keepdims=True)
    i1 = jnp.min(jnp.where(el == v1, lane, big), axis=-1, keepdims=True)
    el2 = jnp.where(lane == i1, neg, el)
    v2 = jnp.max(el2, axis=-1, keepdims=True)
    i2 = jnp.min(jnp.where(el2 == v2, lane, big), axis=-1, keepdims=True)
    e21 = jnp.exp(v2 - v1)
    w1 = p_top / (1.0 + e21)
    w2 = p_top * e21 / (1.0 + e21)
    sel = jnp.where((lane == i1) | (lane == i2), 1.0, 0.0)
    before = _dot(tri_ref[...], sel.astype(BF16)) + carry_ref[...]
    r1 = jnp.sum(jnp.where(lane == i1, before, 0.0), axis=-1, keepdims=True)
    r2 = jnp.sum(jnp.where(lane == i2, before, 0.0), axis=-1, keepdims=True)
    carry = carry_ref[...] + jnp.sum(sel, axis=0, keepdims=True)
    carry_ref[...] = carry
    cnt_ref[...] = jnp.broadcast_to(carry, cnt_ref.shape)
    meta = jnp.where(lane == 0, i1 - groups, jnp.where(lane == 1, i2 - groups,
                     jnp.where(lane == 2, r1, jnp.where(lane == 3, r2, 0.0))))
    meta_ref[...] = meta.astype(jnp.int32)
    wts_ref[...] = jnp.where(lane == 0, w1, jnp.where(lane == 1, w2, 0.0))


def router(h, gain, w_group, b_group, w_expert, b_expert, tm=256):
    n, d = h.shape
    ns = _slab_rows(d)
    groups = w_group.shape[1]
    experts = w_expert.shape[1]
    wr = jnp.zeros((d, LANES), F32).at[:, :groups].set(w_group).at[:, groups:groups + experts].set(w_expert)
    br = jnp.zeros((1, LANES), F32).at[0, :groups].set(b_group).at[0, groups:groups + experts].set(b_expert)
    wr2 = jnp.stack(_split3(wr)[:2], axis=0)
    tri = jnp.asarray(np.tril(np.ones((tm, tm), np.float32), -1), BF16)
    row = lambda i: (i, 0)
    xn, meta, wts, cnt = pl.pallas_call(
        functools.partial(_router_body, groups=groups, per_group=experts // groups),
        grid=(n // tm,),
        in_specs=[pl.BlockSpec((tm, d), row), pl.BlockSpec((1, d), lambda i: (0, 0)),
                  pl.BlockSpec((2, d, LANES), lambda i: (0, 0, 0)), pl.BlockSpec((1, LANES), lambda i: (0, 0)),
                  pl.BlockSpec((tm, tm), lambda i: (0, 0))],
        out_specs=[pl.BlockSpec((tm * ns, LANES), row),
                   pl.BlockSpec((tm, LANES), row), pl.BlockSpec((tm, LANES), row),
                   pl.BlockSpec((SUBLANES, LANES), lambda i: (0, 0))],
        out_shape=[jax.ShapeDtypeStruct((n * ns, LANES), U32),
                   jax.ShapeDtypeStruct((n, LANES), jnp.int32),
                   jax.ShapeDtypeStruct((n, LANES), F32),
                   jax.ShapeDtypeStruct((SUBLANES, LANES), F32)],
        scratch_shapes=[pltpu.VMEM((1, LANES), F32)],
        compiler_params=_cparams(("arbitrary",)),
        name="moe_router",
    )(h, gain.reshape(1, d).astype(F32), wr2, br, tri)
    counts = cnt[0, groups:groups + experts].astype(jnp.int32)
    return xn, meta[:, 0:2], meta[:, 2:4], wts, counts


def _row_copy(src_ref, src_row, dst_ref, dst_row, sem, ns):
    src = src_ref.at[pl.ds(pl.multiple_of(src_row * ns, ns), ns)]
    dst = dst_ref.at[pl.ds(pl.multiple_of(dst_row * ns, ns), ns)]
    return pltpu.make_async_copy(src, dst, sem)


def _pad_bits(tm):
    return [1 << b for b in range(int(math.log2(tm)))]


def _dispatch_body(pad_start_ref, pad_count_ref, nu_ref, pos_ref, x_ref, o_hbm, zero_ref, sem, *, tm, tile,
                   n_experts, ns):
    t = pl.program_id(0)
    half = tile // 2

    @pl.when(t == 0)
    def _():
        zero_ref[...] = jnp.zeros_like(zero_ref)

        def zero_tile(i, c):
            cps = [pltpu.make_async_copy(
                zero_ref, o_hbm.at[pl.ds(pl.multiple_of((i * tile + k * half) * ns, ns), half * ns)], sem)
                for k in range(2)]
            for cp in cps:
                cp.start()
            for cp in cps:
                cp.wait()
            return c
        lax.fori_loop(nu_ref[0], o_hbm.shape[0] // (tile * ns), zero_tile, 0)
        for phase in ("start", "wait"):
            for e in range(n_experts):
                cnt = pad_count_ref[e]
                off = pad_start_ref[e]
                for bit in _pad_bits(tile):
                    dst0 = pl.multiple_of(off * ns, ns)
                    cp = pltpu.make_async_copy(zero_ref.at[pl.ds(0, bit * ns)],
                                               o_hbm.at[pl.ds(dst0, bit * ns)], sem)

                    @pl.when((cnt & bit) != 0)
                    def _():
                        cp.start() if phase == "start" else cp.wait()
                    off = off + (cnt & bit)

    def start(r, c):
        _row_copy(x_ref, r, o_hbm, pos_ref[0, 0, r], sem, ns).start(priority=0)
        _row_copy(x_ref, r, o_hbm, pos_ref[0, 0, tm + r], sem, ns).start(priority=1)
        return c
    lax.fori_loop(0, tm, start, 0)

    def wait(r, c):
        _row_copy(x_ref, 0, o_hbm, 0, sem, ns).wait()
        _row_copy(x_ref, 0, o_hbm, 0, sem, ns).wait()
        return c
    lax.fori_loop(0, tm, wait, 0)


def moe_dispatch(xn, pos, pad_start, pad_count, n_used, p_rows, tile, ns, tm=256):
    n = xn.shape[0] // ns
    nt = n // tm
    n_experts = pad_start.shape[0]
    pos_t = pos.reshape(nt, tm, 2).transpose(0, 2, 1).reshape(nt, 1, 2 * tm)
    return pl.pallas_call(
        functools.partial(_dispatch_body, tm=tm, tile=tile, n_experts=n_experts, ns=ns),
        grid_spec=pltpu.PrefetchScalarGridSpec(
            num_scalar_prefetch=3,
            grid=(nt,),
            in_specs=[pl.BlockSpec((1, 1, 2 * tm), lambda t, ps, pc, nu: (t, 0, 0), memory_space=pltpu.SMEM),
                      pl.BlockSpec((tm * ns, LANES), lambda t, ps, pc, nu: (t, 0))],
            out_specs=pl.BlockSpec(memory_space=pl.ANY),
            scratch_shapes=[pltpu.VMEM((tile // 2 * ns, LANES), xn.dtype), pltpu.SemaphoreType.DMA(())],
        ),
        out_shape=jax.ShapeDtypeStruct((p_rows * ns, LANES), xn.dtype),
        compiler_params=_cparams(("arbitrary",)),
        name="moe_dispatch",
    )(pad_start, pad_count, n_used, pos_t, xn)


def _moe_up_body(te_ref, nu_ref, x_ref, wg_ref, wu_ref, o_ref, wgb_ref, wub_ref):
    t = pl.program_id(1)
    changed = te_ref[t] != te_ref[jnp.maximum(t - 1, 0)]

    @pl.when((t < nu_ref[0]) & ((t == 0) | changed))
    def _():
        wgb_ref[...] = wg_ref[...].astype(BF16)
        wub_ref[...] = wu_ref[...].astype(BF16)

    @pl.when(t < nu_ref[0])
    def _():
        ns = _slab_rows(wgb_ref.shape[0])
        a = None
        u = None
        for s in range(ns):
            x = jnp.concatenate([xs.astype(BF16) for xs in _unpack_rows(x_ref, s, ns)], axis=1)
            rows = slice(2 * s * LANES, (2 * s + 2) * LANES)
            da = _dot(x, wgb_ref[rows, :])
            du = _dot(x, wub_ref[rows, :])
            a = da if a is None else a + da
            u = du if u is None else u + du
        o_ref[...] = (a / (1.0 + jnp.exp(-a)) * u).astype(o_ref.dtype)

    @pl.when(t >= nu_ref[0])
    def _():
        o_ref[...] = jnp.zeros_like(o_ref)


def moe_up(xs, tile_expert, n_used, w_gate_up, layer, tm, fc):
    d, f2 = w_gate_up.shape[2:]
    ns = _slab_rows(d)
    p = xs.shape[0] // ns
    f = f2 // 2
    nc = f // fc
    tile = lambda t, nu: jnp.minimum(t, nu[0] - 1)
    return pl.pallas_call(
        _moe_up_body,
        grid_spec=pltpu.PrefetchScalarGridSpec(
            num_scalar_prefetch=2,
            grid=(nc, p // tm),
            in_specs=[pl.BlockSpec((tm * ns, LANES), lambda c, t, te, nu: (tile(t, nu), 0)),
                      pl.BlockSpec((None, None, d, fc), lambda c, t, te, nu: (layer, te[tile(t, nu)], 0, c)),
                      pl.BlockSpec((None, None, d, fc), lambda c, t, te, nu: (layer, te[tile(t, nu)], 0, nc + c))],
            out_specs=pl.BlockSpec((tm, fc), lambda c, t, te, nu: (t, c)),
            scratch_shapes=[pltpu.VMEM((d, fc), BF16), pltpu.VMEM((d, fc), BF16)],
        ),
        out_shape=jax.ShapeDtypeStruct((p, f), BF16),
        compiler_params=_cparams(("arbitrary", "arbitrary")),
        name="moe_up",
    )(tile_expert, n_used, xs, w_gate_up, w_gate_up)


def _moe_down_body(te_ref, nu_ref, h_ref, w_ref, o_ref, wb_ref):
    t = pl.program_id(0)
    changed = te_ref[t] != te_ref[jnp.maximum(t - 1, 0)]

    @pl.when((t < nu_ref[0]) & ((t == 0) | changed))
    def _():
        wb_ref[...] = w_ref[...].astype(BF16)

    @pl.when(t < nu_ref[0])
    def _():
        _pack_rows(_dot(h_ref[...], wb_ref[...]), o_ref)

    @pl.when(t >= nu_ref[0])
    def _():
        o_ref[...] = jnp.zeros_like(o_ref)


def moe_down(hs, tile_expert, n_used, w_down, layer, tm):
    p, f = hs.shape
    d = w_down.shape[3]
    tile = lambda t, nu: jnp.minimum(t, nu[0] - 1)
    return pl.pallas_call(
        _moe_down_body,
        grid_spec=pltpu.PrefetchScalarGridSpec(
            num_scalar_prefetch=2,
            grid=(p // tm,),
            in_specs=[pl.BlockSpec((tm, f), lambda t, te, nu: (tile(t, nu), 0)),
                      pl.BlockSpec((None, None, f, d), lambda t, te, nu: (layer, te[tile(t, nu)], 0, 0))],
            out_specs=pl.BlockSpec((tm * _slab_rows(d), LANES), lambda t, te, nu: (t, 0)),
            scratch_shapes=[pltpu.VMEM((f, d), BF16)],
        ),
        out_shape=jax.ShapeDtypeStruct((p * _slab_rows(d), LANES), U32),
        compiler_params=_cparams(("arbitrary",)),
        name="moe_down",
    )(tile_expert, n_used, hs, w_down)


def _combine_body(pos_ref, pos_next_ref, h_ref, w_ref, y_hbm, *rest, tm, n_norm):
    if n_norm:
        gain_ref, rest = rest[0], rest[1:]
    o_ref = rest[0]
    norm_refs = rest[1:1 + n_norm]
    buf_ref, sems = rest[1 + n_norm:]
    d = h_ref.shape[1]
    ns = _slab_rows(d)
    rb = 64
    t = pl.program_id(0)
    nt = pl.num_programs(0)

    def fetch(idx_ref, slot):
        def start(r, c):
            _row_copy(y_hbm, idx_ref[0, 0, r], buf_ref.at[slot, 0], r, sems.at[slot], ns).start(priority=0)
            _row_copy(y_hbm, idx_ref[0, 0, tm + r], buf_ref.at[slot, 1], r, sems.at[slot], ns).start(priority=1)
            return c
        lax.fori_loop(0, tm, start, 0)

    def step(slot):
        @pl.when(t == 0)
        def _():
            fetch(pos_ref, slot)

        @pl.when(t + 1 < nt)
        def _():
            fetch(pos_next_ref, 1 - slot)

        def wait(r, c):
            _row_copy(y_hbm, 0, buf_ref.at[slot, 0], r, sems.at[slot], ns).wait()
            _row_copy(y_hbm, 0, buf_ref.at[slot, 1], r, sems.at[slot], ns).wait()
            return c
        lax.fori_loop(0, tm, wait, 0)
        for b0 in range(0, tm, rb):
            rows = slice(b0, b0 + rb)
            w = w_ref[rows, :]
            w0 = jnp.broadcast_to(w[:, 0:1], (rb, LANES))
            w1 = jnp.broadcast_to(w[:, 1:2], (rb, LANES))
            ssq = jnp.zeros((rb, LANES), F32)
            for s in range(ns):
                a_lo, a_hi = _unpack_words(buf_ref[slot, 0, pl.ds(b0 * ns + s, rb, stride=ns), :])
                b_lo, b_hi = _unpack_words(buf_ref[slot, 1, pl.ds(b0 * ns + s, rb, stride=ns), :])
                c_lo = slice((2 * s) * LANES, (2 * s + 1) * LANES)
                c_hi = slice((2 * s + 1) * LANES, (2 * s + 2) * LANES)
                o_lo = h_ref[rows, c_lo] + w0 * a_lo + w1 * b_lo
                o_hi = h_ref[rows, c_hi] + w0 * a_hi + w1 * b_hi
                o_ref[rows, c_lo] = o_lo
                o_ref[rows, c_hi] = o_hi
                if n_norm:
                    ssq = ssq + o_lo * o_lo + o_hi * o_hi
            if n_norm:
                inv = lax.rsqrt(jnp.sum(ssq, axis=-1, keepdims=True) * (1.0 / d) + EPS)
                inv = jnp.broadcast_to(inv, (rb, LANES))
                for j in range(d // LANES):
                    cols = slice(j * LANES, (j + 1) * LANES)
                    y = o_ref[rows, cols] * inv
                    for i, n_ref in enumerate(norm_refs):
                        n_ref[rows, cols] = (y * gain_ref[i:i + 1, cols]).astype(n_ref.dtype)

    for slot in range(2):
        pl.when(t % 2 == slot)(functools.partial(step, slot))


def moe_combine(h, wts, ys, pos, norm_gains=None, tm=256):
    n, d = h.shape
    nt = n // tm
    n_norm = 0 if norm_gains is None else norm_gains.shape[0]
    pos_t = pos.reshape(nt, tm, 2).transpose(0, 2, 1).reshape(nt, 1, 2 * tm)
    row = lambda t: (t, 0)
    in_specs = [pl.BlockSpec((1, 1, 2 * tm), lambda t: (t, 0, 0), memory_space=pltpu.SMEM),
                pl.BlockSpec((1, 1, 2 * tm), lambda t: (jnp.minimum(t + 1, nt - 1), 0, 0),
                             memory_space=pltpu.SMEM),
                pl.BlockSpec((tm, d), row),
                pl.BlockSpec((tm, LANES), row),
                pl.BlockSpec(memory_space=pl.ANY)]
    args = [pos_t, pos_t, h, wts, ys]
    if n_norm:
        in_specs.append(pl.BlockSpec((n_norm, d), lambda t: (0, 0)))
        args.append(norm_gains.astype(F32))
    outs = pl.pallas_call(
        functools.partial(_combine_body, tm=tm, n_norm=n_norm),
        grid=(nt,),
        in_specs=in_specs,
        out_specs=[pl.BlockSpec((tm, d), row)] * (1 + n_norm),
        out_shape=[jax.ShapeDtypeStruct((n, d), F32)] + [jax.ShapeDtypeStruct((n, d), BF16)] * n_norm,
        scratch_shapes=[pltpu.VMEM((2, 2, tm * _slab_rows(d), LANES), ys.dtype),
                        pltpu.SemaphoreType.DMA((2,))],
        compiler_params=_cparams(("arbitrary",)),
        name="moe_combine",
    )(*args)
    return outs[0] if not n_norm else outs


def _routing_tables(ids, ranks, counts, tm):
    n = ids.shape[0]
    n_experts = counts.shape[0]
    padded = ((counts + tm - 1) // tm) * tm
    ends = jnp.cumsum(padded)
    offs = ends - padded
    experts = jnp.arange(n_experts, dtype=jnp.int32)
    pos = ranks + jnp.sum(jnp.where(ids[:, :, None] == experts[None, None, :], offs[None, None, :], 0), axis=-1)
    p_rows = ((2 * n + n_experts * (tm - 1)) // tm) * tm
    tile_start = jnp.arange(p_rows // tm, dtype=jnp.int32) * tm
    tile_expert = jnp.minimum(jnp.sum((ends[None, :] <= tile_start[:, None]).astype(jnp.int32), axis=1),
                              n_experts - 1)
    n_used = (ends[-1:] // tm).astype(jnp.int32)
    return pos.astype(jnp.int32), tile_expert, n_used, (offs + counts).astype(jnp.int32), \
        (padded - counts).astype(jnp.int32), p_rows


def hier_moe_block(h, gain, w_group, b_group, w_expert, b_expert, w_gate_up, w_down, layer, *,
                   tm=MOE_TILE, fc=256, norm_gains=None):
    xn, ids, ranks, wts, counts = router(h, gain, w_group, b_group, w_expert, b_expert)
    pos, tile_expert, n_used, pad_start, pad_count, p_rows = _routing_tables(ids, ranks, counts, tm)
    xs = moe_dispatch(xn, pos, pad_start, pad_count, n_used, p_rows, tm, _slab_rows(h.shape[1]))
    hs = moe_up(xs, tile_expert, n_used, w_gate_up, layer, tm, fc)
    ys = moe_down(hs, tile_expert, n_used, w_down, layer, tm)
    return moe_combine(h, wts, ys, pos, norm_gains)


def kernel(x, a_norm_g, a_w_in, a_w_decay_up, a_b_decay, a_out_norm_g, a_w_out, kv_norm_g, w_kv, k_norm_g,
           b_norm_g, b_w_q, b_q_norm_g, b_sinks, b_w_out, ffn_norm_g, moe_w_group, moe_b_group, moe_w_expert,
           moe_b_expert, moe_w_gate_up, moe_w_down):
    bsz, t, d = x.shape
    n = bsz * t
    h = x.reshape(n, d)
    tm_mm = 1024

    def moe(h, layer, norm_gains=None):
        return hier_moe_block(h, ffn_norm_g[layer], moe_w_group[layer], moe_b_group[layer],
                              moe_w_expert[layer], moe_b_expert[layer], moe_w_gate_up, moe_w_down, layer,
                              norm_gains=norm_gains)

    heads = GLA_HEADS
    dk_all = a_w_decay_up.shape[2]
    dk = dk_all // heads
    dv_all = a_w_out.shape[1]
    dv = dv_all // heads
    n_main = 2 * dk_all + 2 * dv_all
    (xn,) = rmsnorm(h, a_norm_g[0:1])
    w_in = a_w_in[0].astype(BF16)
    proj = matmul(xn, w_in, n_cols=n_main, tm=tm_mm, tn=1024, out_dtype=BF16)
    w_z = jnp.zeros((d, LANES), BF16).at[:, :GLA_GATE_RANK].set(a_w_in[0][:, n_main:].astype(BF16))
    z = matmul(xn, w_z, n_cols=LANES, tm=tm_mm, tn=LANES, out_dtype=F32)
    gated = gla_core(proj, z, a_w_decay_up[0], a_b_decay[0], a_out_norm_g[0],
                     batch=bsz, heads=heads, dk=dk, dv=dv)
    h = matmul(gated, a_w_out[0].astype(BF16), n_cols=d, tm=tm_mm, tn=512, res=h)
    h, xn_kv, xn_b = moe(h, 0, jnp.stack([kv_norm_g, b_norm_g[0]], axis=0))

    hd = SWA_HEAD_DIM
    kvh = SWA_KV_HEADS
    hq = b_w_q.shape[2] // hd
    w_kv_b = w_kv.astype(BF16)
    kw = kvh * hd
    k_sh = matmul(xn_kv, w_kv_b, n_cols=kw, tm=tm_mm, tn=kw, out_dtype=BF16, head_dim=hd, head_gain=k_norm_g,
                  head_major=hd)
    v_sh = matmul(xn_kv, w_kv_b, n_cols=kw, col_block_off=1, tm=tm_mm, tn=kw, out_dtype=BF16, head_major=hd)
    q = matmul(xn_b, b_w_q[0].astype(BF16), n_cols=hq * hd, tm=tm_mm, tn=1024, out_dtype=BF16,
               head_dim=hd, head_gain=b_q_norm_g[0], scale=hd ** -0.5)
    o = swa_attention(q, k_sh, v_sh, b_sinks[0], batch=bsz)
    h = matmul(o, b_w_out[0].astype(BF16), n_cols=d, tm=tm_mm, tn=512, res=h)
    h = moe(h, 1)
    return h.reshape(bsz, t, d)
```

```python
import functools
import math

import numpy as np
import jax
import jax.numpy as jnp
from jax import lax
from jax.experimental import pallas as pl
from jax.experimental.pallas import tpu as pltpu

F32 = jnp.float32
BF16 = jnp.bfloat16
U32 = jnp.uint32
EPS = 1e-6

GLA_HEADS = 8
GLA_GATE_RANK = 16
GLA_GATE_TAU = 16.0
SWA_HEAD_DIM = 64
SWA_KV_HEADS = 8
SWA_WINDOW = 128

LANES = 128
SUBLANES = 8
VMEM_LIMIT_BYTES = 56 * 1024 * 1024

GLA_CHUNK = 64
NEG_BIG = -1e30
MOE_TILE = 256


def _cparams(sem):
    return pltpu.CompilerParams(dimension_semantics=sem, vmem_limit_bytes=VMEM_LIMIT_BYTES)


def _split3(x):
    hi = x.astype(BF16)
    r1 = x - hi.astype(F32)
    mid = r1.astype(BF16)
    lo = (r1 - mid.astype(F32)).astype(BF16)
    return hi, mid, lo


def _dot(a, b):
    return jnp.dot(a, b, preferred_element_type=F32)


def _dot_nt(a, b):
    return lax.dot_general(a, b, (((1,), (1,)), ((), ())), preferred_element_type=F32)


def _dot_tn(a, b):
    return lax.dot_general(a, b, (((0,), (0,)), ((), ())), preferred_element_type=F32)


def _rmsnorm_body(x_ref, g_ref, *o_refs):
    x = x_ref[...]
    y = x * lax.rsqrt(jnp.mean(x * x, axis=-1, keepdims=True) + EPS)
    for i, o_ref in enumerate(o_refs):
        o_ref[...] = (y * g_ref[i:i + 1, :]).astype(o_ref.dtype)


def rmsnorm(x, gains, out_dtype=BF16, tm=256):
    n, d = x.shape
    k = gains.shape[0]
    outs = pl.pallas_call(
        _rmsnorm_body,
        grid=(n // tm,),
        in_specs=[pl.BlockSpec((tm, d), lambda i: (i, 0)), pl.BlockSpec((k, d), lambda i: (0, 0))],
        out_specs=[pl.BlockSpec((tm, d), lambda i: (i, 0)) for _ in range(k)],
        out_shape=[jax.ShapeDtypeStruct((n, d), out_dtype) for _ in range(k)],
        compiler_params=_cparams(("parallel",)),
        name="rmsnorm",
    )(x, gains.astype(F32))
    return outs


def _mm_body(*refs, has_res, head_dim, scale, head_major):
    a_ref, w_ref = refs[0], refs[1]
    pos = 2
    res_ref = None
    if has_res:
        res_ref = refs[pos]
        pos += 1
    if head_dim:
        gain_ref, ind_ref, indt_ref = refs[pos:pos + 3]
        pos += 3
    o_ref = refs[pos]
    y = _dot(a_ref[...], w_ref[...])
    if head_dim:
        ssq = _dot((y * y).astype(BF16), ind_ref[...])
        inv = lax.rsqrt(ssq * (1.0 / head_dim) + EPS)
        hi, mid, _ = _split3(inv)
        it = indt_ref[...]
        y = y * (_dot(hi, it) + _dot(mid, it)) * (gain_ref[...] * scale)
    if has_res:
        y = y + res_ref[...]
    if head_major:
        hd = o_ref.shape[-1]
        for hh in range(o_ref.shape[0]):
            o_ref[hh] = y[:, hh * hd:(hh + 1) * hd].astype(o_ref.dtype)
    else:
        o_ref[...] = y.astype(o_ref.dtype)


def matmul(a, w, *, n_cols, col_block_off=0, tm, tn, res=None, out_dtype=F32,
           head_dim=0, head_gain=None, scale=1.0, head_major=0):
    n, k = a.shape
    assert n % tm == 0 and n_cols % tn == 0
    in_specs = [pl.BlockSpec((tm, k), lambda i, j: (i, 0)),
                pl.BlockSpec((k, tn), lambda i, j: (0, j + col_block_off))]
    args = [a, w]
    if res is not None:
        in_specs.append(pl.BlockSpec((tm, tn), lambda i, j: (i, j)))
        args.append(res)
    if head_dim:
        assert tn % head_dim == 0 and tn // head_dim <= LANES
        nh = tn // head_dim
        ind = np.zeros((tn, LANES), np.float32)
        ind[np.arange(tn), np.arange(tn) // head_dim] = 1.0
        gain_row = jnp.tile(head_gain.astype(F32), nh).reshape(1, tn)
        in_specs += [pl.BlockSpec((1, tn), lambda i, j: (0, 0)),
                     pl.BlockSpec((tn, LANES), lambda i, j: (0, 0)),
                     pl.BlockSpec((LANES, tn), lambda i, j: (0, 0))]
        args += [gain_row, jnp.asarray(ind, BF16), jnp.asarray(ind.T, BF16)]
    if head_major:
        assert tn % head_major == 0
        out_spec = pl.BlockSpec((tn // head_major, tm, head_major), lambda i, j: (j, i, 0))
        out_shape = jax.ShapeDtypeStruct((n_cols // head_major, n, head_major), out_dtype)
    else:
        out_spec = pl.BlockSpec((tm, tn), lambda i, j: (i, j))
        out_shape = jax.ShapeDtypeStruct((n, n_cols), out_dtype)
    return pl.pallas_call(
        functools.partial(_mm_body, has_res=res is not None, head_dim=head_dim, scale=scale,
                          head_major=head_major),
        grid=(n // tm, n_cols // tn),
        in_specs=in_specs,
        out_specs=out_spec,
        out_shape=out_shape,
        compiler_params=_cparams(("parallel", "arbitrary")),
        name="matmul",
    )(*args)


def _gla_tables(c):
    levels = []
    l = c // 2
    while l >= 1:
        levels.append(l)
        l //= 2
    rows = np.arange(c)
    wcum = (rows[None, :] <= rows[:, None]).astype(np.float32)
    wlev = []
    masks = [np.eye(c, dtype=np.float32)]
    for l in levels:
        blk = rows // l
        odd = (blk % 2) == 1
        ref = np.where(odd, blk * l, (blk + 1) * l)
        lo = np.minimum(rows, ref)
        hi = np.maximum(rows, ref)
        wlev.append(((rows[None, :] > lo[:, None]) & (rows[None, :] <= hi[:, None])).astype(np.float32))
        masks.append((odd[:, None] & (blk[None, :] == blk[:, None] - 1)).astype(np.float32))
    return levels, wcum, np.concatenate(wlev, axis=0), np.stack(masks, axis=0)


def _gla_body(q_ref, k_ref, v_ref, r_ref, z_ref, wup_ref, bd_ref, gout_ref, wcum_ref, wlev_ref, mask_ref,
              o_ref, st_ref, e_ref, x_ref, qe_ref, kr_ref, acc_ref, upd_ref, *, c, levels, scale, tau):
    tb, dk = q_ref.shape
    nch = tb // c
    nlev = len(levels)

    @pl.when(pl.program_id(2) == 0)
    def _():
        st_ref[...] = jnp.zeros_like(st_ref)

    zh, zm, _ = _split3(z_ref[...])
    wh, wm = wup_ref[0], wup_ref[1]
    pre = _dot(zh, wh) + _dot(zh, wm) + _dot(zm, wh) + bd_ref[...]
    g = (jnp.minimum(pre, 0.0) - jnp.log(1.0 + jnp.exp(-jnp.abs(pre)))) * (1.0 / tau)
    g_hi = g.astype(BF16)
    g_lo = (g - g_hi.astype(F32)).astype(BF16)
    for ci in range(nch):
        rows = slice(ci * c, (ci + 1) * c)
        ec = _dot(wcum_ref[...], jnp.concatenate([g_hi[rows], g_lo[rows]], axis=1))
        e_ref[0, rows, :] = ec[:, :dk] + ec[:, dk:]
        el = _dot(wlev_ref[...], g_hi[rows])
        for li in range(nlev):
            e_ref[li + 1, rows, :] = el[li * c:(li + 1) * c]
    q = q_ref[...].astype(F32) * scale
    k = k_ref[...].astype(F32)
    ecum = e_ref[0]
    qe_ref[...] = (q * jnp.exp(ecum)).astype(BF16)
    lasts = [ecum[(ci + 1) * c - 1:(ci + 1) * c, :] for ci in range(nch)]
    last_b = jnp.concatenate([jnp.broadcast_to(l, (c, dk)) for l in lasts], axis=0)
    kr_ref[...] = (k * jnp.exp(last_b - ecum)).astype(BF16)
    row = lax.broadcasted_iota(jnp.int32, (tb, dk), 0)
    for li, l in enumerate(levels):
        odd = ((row >> int(math.log2(l))) & 1) == 1
        x_ref[li] = (jnp.where(odd, q, k) * jnp.exp(e_ref[li + 1])).astype(BF16)
    qb = q.astype(BF16)
    kb = k.astype(BF16)
    for ci in range(nch):
        rows = slice(ci * c, (ci + 1) * c)
        a = mask_ref[0] * _dot_nt(qb[rows], kb[rows])
        for li in range(nlev):
            xc = x_ref[li, rows, :]
            a = a + mask_ref[li + 1] * _dot_nt(xc, xc)
        v = v_ref[rows, :]
        acc_ref[rows, :] = _dot(a.astype(BF16), v)
        upd_ref[ci] = _dot_tn(v, kr_ref[rows, :])
    st = st_ref[...]
    for ci in range(nch):
        rows = slice(ci * c, (ci + 1) * c)
        acc_ref[rows, :] = acc_ref[rows, :] + _dot_nt(qe_ref[rows, :], st.astype(BF16))
        st = st * jnp.exp(lasts[ci]) + upd_ref[ci]
    st_ref[...] = st
    o = acc_ref[...]
    o = o * lax.rsqrt(jnp.mean(o * o, axis=-1, keepdims=True) + EPS) * gout_ref[...]
    r = r_ref[...].astype(F32)
    o_ref[...] = (o * (r / (1.0 + jnp.exp(-r)))).astype(o_ref.dtype)


def gla_core(proj, z, w_up, b_decay, g_out, *, batch, heads, dk, dv, tb=512, c=GLA_CHUNK):
    n = proj.shape[0]
    t = n // batch
    nb = t // tb
    levels, wcum, wlev, masks = _gla_tables(c)
    nlev = len(levels)
    rank = w_up.shape[0]
    wup_pad = jnp.zeros((LANES, heads * dk), F32).at[:rank].set(w_up.astype(F32))
    wup2 = jnp.stack(_split3(wup_pad)[:2], axis=0)
    kv0 = 2 * heads * dk // dv
    row_map = lambda b, h, i: b * nb + i
    const2 = lambda b, h, i: (0, 0)
    return pl.pallas_call(
        functools.partial(_gla_body, c=c, levels=levels, scale=dk ** -0.5, tau=GLA_GATE_TAU),
        grid=(batch, heads, nb),
        in_specs=[
            pl.BlockSpec((tb, dk), lambda b, h, i: (row_map(b, h, i), h)),
            pl.BlockSpec((tb, dk), lambda b, h, i: (row_map(b, h, i), heads + h)),
            pl.BlockSpec((tb, dv), lambda b, h, i: (row_map(b, h, i), kv0 + h)),
            pl.BlockSpec((tb, dv), lambda b, h, i: (row_map(b, h, i), kv0 + heads + h)),
            pl.BlockSpec((tb, LANES), lambda b, h, i: (row_map(b, h, i), 0)),
            pl.BlockSpec((2, LANES, dk), lambda b, h, i: (0, 0, h)),
            pl.BlockSpec((1, dk), lambda b, h, i: (0, h)),
            pl.BlockSpec((1, dv), const2),
            pl.BlockSpec(wcum.shape, const2),
            pl.BlockSpec(wlev.shape, const2),
            pl.BlockSpec(masks.shape, lambda b, h, i: (0, 0, 0)),
        ],
        out_specs=pl.BlockSpec((tb, dv), lambda b, h, i: (row_map(b, h, i), h)),
        out_shape=jax.ShapeDtypeStruct((n, heads * dv), BF16),
        scratch_shapes=[pltpu.VMEM((dv, dk), F32),
                        pltpu.VMEM((nlev + 1, tb, dk), F32),
                        pltpu.VMEM((nlev, tb, dk), BF16),
                        pltpu.VMEM((tb, dk), BF16),
                        pltpu.VMEM((tb, dk), BF16),
                        pltpu.VMEM((tb, dv), F32),
                        pltpu.VMEM((tb // c, dv, dk), F32)],
        compiler_params=_cparams(("parallel", "parallel", "arbitrary")),
        name="gla_core",
    )(proj, proj, proj, proj, z, wup2, b_decay.reshape(1, -1).astype(F32), g_out.reshape(1, -1).astype(F32),
      jnp.asarray(wcum, BF16), jnp.asarray(wlev, BF16), jnp.asarray(masks, F32))


def _swa_body(q_ref, kc_ref, kp_ref, vc_ref, vp_ref, sink_ref, bias_ref, qg_ref, o_ref, *, group, blk):
    kvh, _, hd = kc_ref.shape
    gw = group * hd
    ones = jnp.ones((SUBLANES, hd), BF16)
    for h in range(kvh):
        qh = q_ref[:, h * gw:(h + 1) * gw]
        q_all = jnp.concatenate([qh[:, g * hd:(g + 1) * hd] for g in range(group)], axis=0)
        qf = q_all.astype(F32)
        ssq = _dot_nt(ones, (qf * qf).astype(BF16))[0:1]
        inv_q = lax.rsqrt(ssq * (1.0 / hd) + EPS)
        k2 = jnp.concatenate([kp_ref[h], kc_ref[h]], axis=0)
        v2 = jnp.concatenate([vp_ref[h], vc_ref[h]], axis=0)
        k2g = (k2.astype(F32) * qg_ref[...]).astype(BF16)
        s = _dot_nt(k2g, q_all) * inv_q + bias_ref[...]
        sink = sink_ref[h]
        m = jnp.maximum(jnp.max(s, axis=0, keepdims=True), sink)
        p = jnp.exp(s - m)
        denom = jnp.sum(p, axis=0, keepdims=True) + jnp.exp(sink - m)
        o_t = _dot_tn(v2, p.astype(BF16)) * (1.0 / denom)
        o = jnp.concatenate([o_t[:, g * blk:(g + 1) * blk].T for g in range(group)], axis=1)
        o_ref[:, h * gw:(h + 1) * gw] = o.astype(o_ref.dtype)


def swa_attention(q, k, v, sinks, q_gain, *, batch, blk=SWA_WINDOW):
    n, qw = q.shape
    kvh, _, hd = k.shape
    group = qw // hd // kvh
    nb = n // batch // blk
    ki = np.arange(2 * blk)[:, None]
    qi = np.tile(np.arange(blk), group)[None, :]
    dist = blk + qi - ki
    band = (dist >= 0) & (dist < blk)
    bias = np.stack([np.where(band & (ki >= blk), 0.0, NEG_BIG), np.where(band, 0.0, NEG_BIG)], axis=0)
    sink_rows = jnp.repeat(sinks.astype(F32).reshape(kvh, group), blk, axis=1).reshape(kvh, 1, group * blk)
    row = lambda b, j: b * nb + j
    prev = lambda b, j: (0, b * nb + jnp.maximum(j - 1, 0), 0)
    cur = lambda b, j: (0, row(b, j), 0)
    kv_c = pl.BlockSpec((kvh, blk, hd), cur)
    kv_p = pl.BlockSpec((kvh, blk, hd), prev)
    return pl.pallas_call(
        functools.partial(_swa_body, group=group, blk=blk),
        grid=(batch, nb),
        in_specs=[
            pl.BlockSpec((blk, qw), lambda b, j: (row(b, j), 0)),
            kv_c, kv_p, kv_c, kv_p,
            pl.BlockSpec((kvh, 1, group * blk), lambda b, j: (0, 0, 0)),
            pl.BlockSpec((None, 2 * blk, group * blk), lambda b, j: (jnp.minimum(j, 1), 0, 0)),
            pl.BlockSpec((1, hd), lambda b, j: (0, 0)),
        ],
        out_specs=pl.BlockSpec((blk, qw), lambda b, j: (row(b, j), 0)),
        out_shape=jax.ShapeDtypeStruct((n, qw), BF16),
        compiler_params=_cparams(("parallel", "arbitrary")),
        name="swa_attention",
    )(q, k, k, v, v, sink_rows, jnp.asarray(bias.astype(np.float32)), q_gain.astype(F32).reshape(1, hd))


def _pack_rows(y, o_ref):
    rows, d = y.shape
    ns = _slab_rows(d)
    for s in range(ns):
        lo = y[:, (2 * s) * LANES:(2 * s + 1) * LANES]
        hi = y[:, (2 * s + 1) * LANES:(2 * s + 2) * LANES]
        lo_bits = lax.bitcast_convert_type(lo.astype(BF16).astype(F32), U32)
        hi_bits = lax.bitcast_convert_type(hi.astype(BF16).astype(F32), U32)
        o_ref[pl.ds(s, rows, stride=ns), :] = (lo_bits >> 16) | (hi_bits & jnp.uint32(0xFFFF0000))


def _slab_rows(d):
    assert d % (2 * SUBLANES * LANES) == 0
    return d // (2 * LANES)


def _unpack_words(w):
    lo = lax.bitcast_convert_type(w << 16, F32)
    hi = lax.bitcast_convert_type(w & jnp.uint32(0xFFFF0000), F32)
    return lo, hi


def _unpack_rows(x_ref, s, ns):
    return _unpack_words(x_ref[pl.ds(s, x_ref.shape[0] // ns, stride=ns), :])


def _router_body(x_ref, g_ref, wr_ref, br_ref, tri_ref, xn_ref, meta_ref, wts_ref, cnt_ref, carry_ref,
                 *, groups, per_group):
    @pl.when(pl.program_id(0) == 0)
    def _():
        carry_ref[...] = jnp.zeros_like(carry_ref)

    x = x_ref[...]
    xn = x * lax.rsqrt(jnp.mean(x * x, axis=-1, keepdims=True) + EPS) * g_ref[...]
    _pack_rows(xn, xn_ref)
    xh, xm, _ = _split3(xn)
    wh, wm = wr_ref[0], wr_ref[1]
    logits = _dot(xh, wh) + _dot(xh, wm) + _dot(xm, wh) + br_ref[...]
    lane = lax.broadcasted_iota(jnp.int32, logits.shape, 1).astype(F32)
    big = jnp.float32(LANES)
    neg = jnp.float32(-jnp.inf)
    gl = jnp.where(lane < groups, logits, neg)
    gm = jnp.max(gl, axis=-1, keepdims=True)
    p_top = 1.0 / jnp.sum(jnp.exp(gl - gm), axis=-1, keepdims=True)
    g_idx = jnp.min(jnp.where(gl == gm, lane, big), axis=-1, keepdims=True)
    lo = groups + g_idx * per_group
    el = jnp.where((lane >= lo) & (lane < lo + per_group), logits, neg)
    v1 = jnp.max(el, axis=-1, keepdims=True)
    i1 = jnp.min(jnp.where(el == v1, lane, big), axis=-1, keepdims=True)
    el2 = jnp.where(lane == i1, neg, el)
    v2 = jnp.max(el2, axis=-1, keepdims=True)
    i2 = jnp.min(jnp.where(el2 == v2, lane, big), axis=-1, keepdims=True)
    e21 = jnp.exp(v2 - v1)
    w1 = p_top / (1.0 + e21)
    w2 = p_top * e21 / (1.0 + e21)
    sel = jnp.where((lane == i1) | (lane == i2), 1.0, 0.0)
    before = _dot(tri_ref[...], sel.astype(BF16)) + carry_ref[...]
    r1 = jnp.sum(jnp.where(lane == i1, before, 0.0), axis=-1, keepdims=True)
    r2 = jnp.sum(jnp.where(lane == i2, before, 0.0), axis=-1, keepdims=True)
    carry = carry_ref[...] + jnp.sum(sel, axis=0, keepdims=True)
    carry_ref[...] = carry
    cnt_ref[...] = jnp.broadcast_to(carry, cnt_ref.shape)
    meta = jnp.where(lane == 0, i1 - groups, jnp.where(lane == 1, i2 - groups,
                     jnp.where(lane == 2, r1, jnp.where(lane == 3, r2, 0.0))))
    meta_ref[...] = meta.astype(jnp.int32)
    wts_ref[...] = jnp.where(lane == 0, w1, jnp.where(lane == 1, w2, 0.0))


def router(h, gain, w_group, b_group, w_expert, b_expert, tm=256):
    n, d = h.shape
    ns = _slab_rows(d)
    groups = w_group.shape[1]
    experts = w_expert.shape[1]
    wr = jnp.zeros((d, LANES), F32).at[:, :groups].set(w_group).at[:, groups:groups + experts].set(w_expert)
    br = jnp.zeros((1, LANES), F32).at[0, :groups].set(b_group).at[0, groups:groups + experts].set(b_expert)
    wr2 = jnp.stack(_split3(wr)[:2], axis=0)
    tri = jnp.asarray(np.tril(np.ones((tm, tm), np.float32), -1), BF16)
    row = lambda i: (i, 0)
    xn, meta, wts, cnt = pl.pallas_call(
        functools.partial(_router_body, groups=groups, per_group=experts // groups),
        grid=(n // tm,),
        in_specs=[pl.BlockSpec((tm, d), row), pl.BlockSpec((1, d), lambda i: (0, 0)),
                  pl.BlockSpec((2, d, LANES), lambda i: (0, 0, 0)), pl.BlockSpec((1, LANES), lambda i: (0, 0)),
                  pl.BlockSpec((tm, tm), lambda i: (0, 0))],
        out_specs=[pl.BlockSpec((tm * ns, LANES), row),
                   pl.BlockSpec((tm, LANES), row), pl.BlockSpec((tm, LANES), row),
                   pl.BlockSpec((SUBLANES, LANES), lambda i: (0, 0))],
        out_shape=[jax.ShapeDtypeStruct((n * ns, LANES), U32),
                   jax.ShapeDtypeStruct((n, LANES), jnp.int32),
                   jax.ShapeDtypeStruct((n, LANES), F32),
                   jax.ShapeDtypeStruct((SUBLANES, LANES), F32)],
        scratch_shapes=[pltpu.VMEM((1, LANES), F32)],
        compiler_params=_cparams(("arbitrary",)),
        name="moe_router",
    )(h, gain.reshape(1, d).astype(F32), wr2, br, tri)
    counts = cnt[0, groups:groups + experts].astype(jnp.int32)
    return xn, meta[:, 0:2], meta[:, 2:4], wts, counts


def _row_copy(src_ref, src_row, dst_ref, dst_row, sem, ns):
    src = src_ref.at[pl.ds(pl.multiple_of(src_row * ns, ns), ns)]
    dst = dst_ref.at[pl.ds(pl.multiple_of(dst_row * ns, ns), ns)]
    return pltpu.make_async_copy(src, dst, sem)


def _pad_bits(tm):
    return [1 << b for b in range(int(math.log2(tm)))]


def _dispatch_body(pad_start_ref, pad_count_ref, nu_ref, pos_ref, x_ref, o_hbm, zero_ref, sem, *, tm, tile,
                   n_experts, ns):
    t = pl.program_id(0)
    half = tile // 2

    @pl.when(t == 0)
    def _():
        zero_ref[...] = jnp.zeros_like(zero_ref)

        def zero_tile(i, c):
            cps = [pltpu.make_async_copy(
                zero_ref, o_hbm.at[pl.ds(pl.multiple_of((i * tile + k * half) * ns, ns), half * ns)], sem)
                for k in range(2)]
            for cp in cps:
                cp.start()
            for cp in cps:
                cp.wait()
            return c
        lax.fori_loop(nu_ref[0], o_hbm.shape[0] // (tile * ns), zero_tile, 0)
        for phase in ("start", "wait"):
            for e in range(n_experts):
                cnt = pad_count_ref[e]
                off = pad_start_ref[e]
                for bit in _pad_bits(tile):
                    dst0 = pl.multiple_of(off * ns, ns)
                    cp = pltpu.make_async_copy(zero_ref.at[pl.ds(0, bit * ns)],
                                               o_hbm.at[pl.ds(dst0, bit * ns)], sem)

                    @pl.when((cnt & bit) != 0)
                    def _():
                        cp.start() if phase == "start" else cp.wait()
                    off = off + (cnt & bit)

    def start(r, c):
        _row_copy(x_ref, r, o_hbm, pos_ref[0, 0, r], sem, ns).start(priority=0)
        _row_copy(x_ref, r, o_hbm, pos_ref[0, 0, tm + r], sem, ns).start(priority=1)
        return c
    lax.fori_loop(0, tm, start, 0)

    def wait(r, c):
        _row_copy(x_ref, 0, o_hbm, 0, sem, ns).wait()
        _row_copy(x_ref, 0, o_hbm, 0, sem, ns).wait()
        return c
    lax.fori_loop(0, tm, wait, 0)


def moe_dispatch(xn, pos, pad_start, pad_count, n_used, p_rows, tile, ns, tm=256):
    n = xn.shape[0] // ns
    nt = n // tm
    n_experts = pad_start.shape[0]
    pos_t = pos.reshape(nt, tm, 2).transpose(0, 2, 1).reshape(nt, 1, 2 * tm)
    return pl.pallas_call(
        functools.partial(_dispatch_body, tm=tm, tile=tile, n_experts=n_experts, ns=ns),
        grid_spec=pltpu.PrefetchScalarGridSpec(
            num_scalar_prefetch=3,
            grid=(nt,),
            in_specs=[pl.BlockSpec((1, 1, 2 * tm), lambda t, ps, pc, nu: (t, 0, 0), memory_space=pltpu.SMEM),
                      pl.BlockSpec((tm * ns, LANES), lambda t, ps, pc, nu: (t, 0))],
            out_specs=pl.BlockSpec(memory_space=pl.ANY),
            scratch_shapes=[pltpu.VMEM((tile // 2 * ns, LANES), xn.dtype), pltpu.SemaphoreType.DMA(())],
        ),
        out_shape=jax.ShapeDtypeStruct((p_rows * ns, LANES), xn.dtype),
        compiler_params=_cparams(("arbitrary",)),
        name="moe_dispatch",
    )(pad_start, pad_count, n_used, pos_t, xn)


def _moe_up_body(te_ref, nu_ref, x_ref, wg_ref, wu_ref, o_ref, wgb_ref, wub_ref):
    t = pl.program_id(1)
    changed = te_ref[t] != te_ref[jnp.maximum(t - 1, 0)]

    @pl.when((t < nu_ref[0]) & ((t == 0) | changed))
    def _():
        wgb_ref[...] = wg_ref[...].astype(BF16)
        wub_ref[...] = wu_ref[...].astype(BF16)

    @pl.when(t < nu_ref[0])
    def _():
        ns = _slab_rows(wgb_ref.shape[0])
        a = None
        u = None
        for s in range(ns):
            x = jnp.concatenate([xs.astype(BF16) for xs in _unpack_rows(x_ref, s, ns)], axis=1)
            rows = slice(2 * s * LANES, (2 * s + 2) * LANES)
            da = _dot(x, wgb_ref[rows, :])
            du = _dot(x, wub_ref[rows, :])
            a = da if a is None else a + da
            u = du if u is None else u + du
        o_ref[...] = (a / (1.0 + jnp.exp(-a)) * u).astype(o_ref.dtype)

    @pl.when(t >= nu_ref[0])
    def _():
        o_ref[...] = jnp.zeros_like(o_ref)


def moe_up(xs, tile_expert, n_used, w_gate_up, layer, tm, fc):
    d, f2 = w_gate_up.shape[2:]
    ns = _slab_rows(d)
    p = xs.shape[0] // ns
    f = f2 // 2
    nc = f // fc
    tile = lambda t, nu: jnp.minimum(t, nu[0] - 1)
    return pl.pallas_call(
        _moe_up_body,
        grid_spec=pltpu.PrefetchScalarGridSpec(
            num_scalar_prefetch=2,
            grid=(nc, p // tm),
            in_specs=[pl.BlockSpec((tm * ns, LANES), lambda c, t, te, nu: (tile(t, nu), 0)),
                      pl.BlockSpec((None, None, d, fc), lambda c, t, te, nu: (layer, te[tile(t, nu)], 0, c)),
                      pl.BlockSpec((None, None, d, fc), lambda c, t, te, nu: (layer, te[tile(t, nu)], 0, nc + c))],
            out_specs=pl.BlockSpec((tm, fc), lambda c, t, te, nu: (t, c)),
            scratch_shapes=[pltpu.VMEM((d, fc), BF16), pltpu.VMEM((d, fc), BF16)],
        ),
        out_shape=jax.ShapeDtypeStruct((p, f), BF16),
        compiler_params=_cparams(("arbitrary", "arbitrary")),
        name="moe_up",
    )(tile_expert, n_used, xs, w_gate_up, w_gate_up)


def _moe_down_body(te_ref, nu_ref, h_ref, w_ref, o_ref, wb_ref):
    t = pl.program_id(0)
    changed = te_ref[t] != te_ref[jnp.maximum(t - 1, 0)]

    @pl.when((t < nu_ref[0]) & ((t == 0) | changed))
    def _():
        wb_ref[...] = w_ref[...].astype(BF16)

    @pl.when(t < nu_ref[0])
    def _():
        _pack_rows(_dot(h_ref[...], wb_ref[...]), o_ref)

    @pl.when(t >= nu_ref[0])
    def _():
        o_ref[...] = jnp.zeros_like(o_ref)


def moe_down(hs, tile_expert, n_used, w_down, layer, tm):
    p, f = hs.shape
    d = w_down.shape[3]
    tile = lambda t, nu: jnp.minimum(t, nu[0] - 1)
    return pl.pallas_call(
        _moe_down_body,
        grid_spec=pltpu.PrefetchScalarGridSpec(
            num_scalar_prefetch=2,
            grid=(p // tm,),
            in_specs=[pl.BlockSpec((tm, f), lambda t, te, nu: (tile(t, nu), 0)),
                      pl.BlockSpec((None, None, f, d), lambda t, te, nu: (layer, te[tile(t, nu)], 0, 0))],
            out_specs=pl.BlockSpec((tm * _slab_rows(d), LANES), lambda t, te, nu: (t, 0)),
            scratch_shapes=[pltpu.VMEM((f, d), BF16)],
        ),
        out_shape=jax.ShapeDtypeStruct((p * _slab_rows(d), LANES), U32),
        compiler_params=_cparams(("arbitrary",)),
        name="moe_down",
    )(tile_expert, n_used, hs, w_down)


def _combine_body(pos_ref, pos_next_ref, h_ref, w_ref, y_hbm, *rest, tm, n_norm):
    if n_norm:
        gain_ref, rest = rest[0], rest[1:]
    o_ref = rest[0]
    norm_refs = rest[1:1 + n_norm]
    buf_ref, sems = rest[1 + n_norm:]
    d = h_ref.shape[1]
    ns = _slab_rows(d)
    rb = 64
    t = pl.program_id(0)
    nt = pl.num_programs(0)

    def fetch(idx_ref, slot):
        def start(r, c):
            _row_copy(y_hbm, idx_ref[0, 0, r], buf_ref.at[slot, 0], r, sems.at[slot], ns).start(priority=0)
            _row_copy(y_hbm, idx_ref[0, 0, tm + r], buf_ref.at[slot, 1], r, sems.at[slot], ns).start(priority=1)
            return c
        lax.fori_loop(0, tm, start, 0)

    def step(slot):
        @pl.when(t == 0)
        def _():
            fetch(pos_ref, slot)

        @pl.when(t + 1 < nt)
        def _():
            fetch(pos_next_ref, 1 - slot)

        def wait(r, c):
            _row_copy(y_hbm, 0, buf_ref.at[slot, 0], r, sems.at[slot], ns).wait()
            _row_copy(y_hbm, 0, buf_ref.at[slot, 1], r, sems.at[slot], ns).wait()
            return c
        lax.fori_loop(0, tm, wait, 0)
        for b0 in range(0, tm, rb):
            rows = slice(b0, b0 + rb)
            w = w_ref[rows, :]
            w0 = jnp.broadcast_to(w[:, 0:1], (rb, LANES))
            w1 = jnp.broadcast_to(w[:, 1:2], (rb, LANES))
            ssq = jnp.zeros((rb, LANES), F32)
            for s in range(ns):
                a_lo, a_hi = _unpack_words(buf_ref[slot, 0, pl.ds(b0 * ns + s, rb, stride=ns), :])
                b_lo, b_hi = _unpack_words(buf_ref[slot, 1, pl.ds(b0 * ns + s, rb, stride=ns), :])
                c_lo = slice((2 * s) * LANES, (2 * s + 1) * LANES)
                c_hi = slice((2 * s + 1) * LANES, (2 * s + 2) * LANES)
                o_lo = h_ref[rows, c_lo] + w0 * a_lo + w1 * b_lo
                o_hi = h_ref[rows, c_hi] + w0 * a_hi + w1 * b_hi
                o_ref[rows, c_lo] = o_lo
                o_ref[rows, c_hi] = o_hi
                if n_norm:
                    ssq = ssq + o_lo * o_lo + o_hi * o_hi
            if n_norm:
                inv = lax.rsqrt(jnp.sum(ssq, axis=-1, keepdims=True) * (1.0 / d) + EPS)
                inv = jnp.broadcast_to(inv, (rb, LANES))
                for j in range(d // LANES):
                    cols = slice(j * LANES, (j + 1) * LANES)
                    y = o_ref[rows, cols] * inv
                    for i, n_ref in enumerate(norm_refs):
                        n_ref[rows, cols] = (y * gain_ref[i:i + 1, cols]).astype(n_ref.dtype)

    for slot in range(2):
        pl.when(t % 2 == slot)(functools.partial(step, slot))


def moe_combine(h, wts, ys, pos, norm_gains=None, tm=256):
    n, d = h.shape
    nt = n // tm
    n_norm = 0 if norm_gains is None else norm_gains.shape[0]
    pos_t = pos.reshape(nt, tm, 2).transpose(0, 2, 1).reshape(nt, 1, 2 * tm)
    row = lambda t: (t, 0)
    in_specs = [pl.BlockSpec((1, 1, 2 * tm), lambda t: (t, 0, 0), memory_space=pltpu.SMEM),
                pl.BlockSpec((1, 1, 2 * tm), lambda t: (jnp.minimum(t + 1, nt - 1), 0, 0),
                             memory_space=pltpu.SMEM),
                pl.BlockSpec((tm, d), row),
                pl.BlockSpec((tm, LANES), row),
                pl.BlockSpec(memory_space=pl.ANY)]
    args = [pos_t, pos_t, h, wts, ys]
    if n_norm:
        in_specs.append(pl.BlockSpec((n_norm, d), lambda t: (0, 0)))
        args.append(norm_gains.astype(F32))
    outs = pl.pallas_call(
        functools.partial(_combine_body, tm=tm, n_norm=n_norm),
        grid=(nt,),
        in_specs=in_specs,
        out_specs=[pl.BlockSpec((tm, d), row)] * (1 + n_norm),
        out_shape=[jax.ShapeDtypeStruct((n, d), F32)] + [jax.ShapeDtypeStruct((n, d), BF16)] * n_norm,
        scratch_shapes=[pltpu.VMEM((2, 2, tm * _slab_rows(d), LANES), ys.dtype),
                        pltpu.SemaphoreType.DMA((2,))],
        compiler_params=_cparams(("arbitrary",)),
        name="moe_combine",
    )(*args)
    return outs[0] if not n_norm else outs


def _routing_tables(ids, ranks, counts, tm):
    n = ids.shape[0]
    n_experts = counts.shape[0]
    padded = ((counts + tm - 1) // tm) * tm
    ends = jnp.cumsum(padded)
    offs = ends - padded
    experts = jnp.arange(n_experts, dtype=jnp.int32)
    pos = ranks + jnp.sum(jnp.where(ids[:, :, None] == experts[None, None, :], offs[None, None, :], 0), axis=-1)
    p_rows = ((2 * n + n_experts * (tm - 1)) // tm) * tm
    tile_start = jnp.arange(p_rows // tm, dtype=jnp.int32) * tm
    tile_expert = jnp.minimum(jnp.sum((ends[None, :] <= tile_start[:, None]).astype(jnp.int32), axis=1),
                              n_experts - 1)
    n_used = (ends[-1:] // tm).astype(jnp.int32)
    return pos.astype(jnp.int32), tile_expert, n_used, (offs + counts).astype(jnp.int32), \
        (padded - counts).astype(jnp.int32), p_rows


def hier_moe_block(h, gain, w_group, b_group, w_expert, b_expert, w_gate_up, w_down, layer, *,
                   tm=MOE_TILE, fc=512, norm_gains=None):
    xn, ids, ranks, wts, counts = router(h, gain, w_group, b_group, w_expert, b_expert)
    pos, tile_expert, n_used, pad_start, pad_count, p_rows = _routing_tables(ids, ranks, counts, tm)
    xs = moe_dispatch(xn, pos, pad_start, pad_count, n_used, p_rows, tm, _slab_rows(h.shape[1]))
    hs = moe_up(xs, tile_expert, n_used, w_gate_up, layer, tm, fc)
    ys = moe_down(hs, tile_expert, n_used, w_down, layer, tm)
    return moe_combine(h, wts, ys, pos, norm_gains)


def kernel(x, a_norm_g, a_w_in, a_w_decay_up, a_b_decay, a_out_norm_g, a_w_out, kv_norm_g, w_kv, k_norm_g,
           b_norm_g, b_w_q, b_q_norm_g, b_sinks, b_w_out, ffn_norm_g, moe_w_group, moe_b_group, moe_w_expert,
           moe_b_expert, moe_w_gate_up, moe_w_down):
    bsz, t, d = x.shape
    n = bsz * t
    h = x.reshape(n, d)
    tm_mm = 1024

    def moe(h, layer, norm_gains=None):
        return hier_moe_block(h, ffn_norm_g[layer], moe_w_group[layer], moe_b_group[layer],
                              moe_w_expert[layer], moe_b_expert[layer], moe_w_gate_up, moe_w_down, layer,
                              norm_gains=norm_gains)

    heads = GLA_HEADS
    dk_all = a_w_decay_up.shape[2]
    dk = dk_all // heads
    dv_all = a_w_out.shape[1]
    dv = dv_all // heads
    n_main = 2 * dk_all + 2 * dv_all
    (xn,) = rmsnorm(h, a_norm_g[0:1])
    w_in = a_w_in[0].astype(BF16)
    proj = matmul(xn, w_in, n_cols=n_main, tm=tm_mm, tn=1024, out_dtype=BF16)
    w_z = jnp.zeros((d, LANES), BF16).at[:, :GLA_GATE_RANK].set(a_w_in[0][:, n_main:].astype(BF16))
    z = matmul(xn, w_z, n_cols=LANES, tm=tm_mm, tn=LANES, out_dtype=F32)
    gated = gla_core(proj, z, a_w_decay_up[0], a_b_decay[0], a_out_norm_g[0],
                     batch=bsz, heads=heads, dk=dk, dv=dv)
    h = matmul(gated, a_w_out[0].astype(BF16), n_cols=d, tm=tm_mm, tn=512, res=h)
    h, xn_kv, xn_b = moe(h, 0, jnp.stack([kv_norm_g, b_norm_g[0]], axis=0))

    hd = SWA_HEAD_DIM
    kvh = SWA_KV_HEADS
    hq = b_w_q.shape[2] // hd
    w_kv_b = w_kv.astype(BF16)
    kw = kvh * hd
    k_sh = matmul(xn_kv, w_kv_b, n_cols=kw, tm=tm_mm, tn=kw, out_dtype=BF16, head_dim=hd, head_gain=k_norm_g,
                  head_major=hd)
    v_sh = matmul(xn_kv, w_kv_b, n_cols=kw, col_block_off=1, tm=tm_mm, tn=kw, out_dtype=BF16, head_major=hd)
    q = matmul(xn_b, b_w_q[0].astype(BF16), n_cols=hq * hd, tm=tm_mm, tn=1024, out_dtype=BF16)
    o = swa_attention(q, k_sh, v_sh, b_sinks[0], b_q_norm_g[0] * hd ** -0.5, batch=bsz)
    h = matmul(o, b_w_out[0].astype(BF16), n_cols=d, tm=tm_mm, tn=512, res=h)
    h = moe(h, 1)
    return h.reshape(bsz, t, d)
```

```python
import functools
import math

import numpy as np
import jax
import jax.numpy as jnp
from jax import lax
from jax.experimental import pallas as pl
from jax.experimental.pallas import tpu as pltpu

F32 = jnp.float32
BF16 = jnp.bfloat16
U32 = jnp.uint32
EPS = 1e-6

GLA_HEADS = 8
GLA_GATE_RANK = 16
GLA_GATE_TAU = 16.0
SWA_HEAD_DIM = 64
SWA_KV_HEADS = 8
SWA_WINDOW = 128

LANES = 128
SUBLANES = 8
VMEM_LIMIT_BYTES = 56 * 1024 * 1024

GLA_CHUNK = 64
NEG_BIG = -1e30
MOE_TILE = 256


def _cparams(sem):
    return pltpu.CompilerParams(dimension_semantics=sem, vmem_limit_bytes=VMEM_LIMIT_BYTES)


def _split3(x):
    hi = x.astype(BF16)
    r1 = x - hi.astype(F32)
    mid = r1.astype(BF16)
    lo = (r1 - mid.astype(F32)).astype(BF16)
    return hi, mid, lo


def _dot(a, b):
    return jnp.dot(a, b, preferred_element_type=F32)


def _dot_nt(a, b):
    return lax.dot_general(a, b, (((1,), (1,)), ((), ())), preferred_element_type=F32)


def _dot_tn(a, b):
    return lax.dot_general(a, b, (((0,), (0,)), ((), ())), preferred_element_type=F32)


def _rmsnorm_body(x_ref, g_ref, *refs, n_out, has_proj):
    if has_proj:
        w_ref, refs = refs[0], refs[1:]
    x = x_ref[...]
    y = x * lax.rsqrt(jnp.mean(x * x, axis=-1, keepdims=True) + EPS)
    for i in range(n_out):
        yi = (y * g_ref[i:i + 1, :]).astype(refs[i].dtype)
        refs[i][...] = yi
        if has_proj and i == 0:
            refs[n_out][...] = _dot(yi, w_ref[...])


def rmsnorm(x, gains, out_dtype=BF16, tm=256, proj_w=None):
    n, d = x.shape
    k = gains.shape[0]
    row = lambda i: (i, 0)
    in_specs = [pl.BlockSpec((tm, d), row), pl.BlockSpec((k, d), lambda i: (0, 0))]
    out_specs = [pl.BlockSpec((tm, d), row) for _ in range(k)]
    out_shape = [jax.ShapeDtypeStruct((n, d), out_dtype) for _ in range(k)]
    args = [x, gains.astype(F32)]
    if proj_w is not None:
        m = proj_w.shape[1]
        in_specs.append(pl.BlockSpec((d, m), lambda i: (0, 0)))
        out_specs.append(pl.BlockSpec((tm, m), row))
        out_shape.append(jax.ShapeDtypeStruct((n, m), F32))
        args.append(proj_w)
    outs = pl.pallas_call(
        functools.partial(_rmsnorm_body, n_out=k, has_proj=proj_w is not None),
        grid=(n // tm,),
        in_specs=in_specs,
        out_specs=out_specs,
        out_shape=out_shape,
        compiler_params=_cparams(("parallel",)),
        name="rmsnorm",
    )(*args)
    return outs


def _mm_body(*refs, has_res, head_dim, scale, head_major):
    a_ref, w_ref = refs[0], refs[1]
    pos = 2
    res_ref = None
    if has_res:
        res_ref = refs[pos]
        pos += 1
    if head_dim:
        gain_ref, ind_ref, indt_ref = refs[pos:pos + 3]
        pos += 3
    o_ref = refs[pos]
    y = _dot(a_ref[...], w_ref[...])
    if head_dim:
        ssq = _dot((y * y).astype(BF16), ind_ref[...])
        inv = lax.rsqrt(ssq * (1.0 / head_dim) + EPS)
        hi, mid, _ = _split3(inv)
        it = indt_ref[...]
        y = y * (_dot(hi, it) + _dot(mid, it)) * (gain_ref[...] * scale)
    if has_res:
        y = y + res_ref[...]
    if head_major:
        hd = o_ref.shape[-1]
        for hh in range(o_ref.shape[0]):
            o_ref[hh] = y[:, hh * hd:(hh + 1) * hd].astype(o_ref.dtype)
    else:
        o_ref[...] = y.astype(o_ref.dtype)


def matmul(a, w, *, n_cols, col_block_off=0, tm, tn, res=None, out_dtype=F32,
           head_dim=0, head_gain=None, scale=1.0, head_major=0):
    n, k = a.shape
    assert n % tm == 0 and n_cols % tn == 0
    in_specs = [pl.BlockSpec((tm, k), lambda i, j: (i, 0)),
                pl.BlockSpec((k, tn), lambda i, j: (0, j + col_block_off))]
    args = [a, w]
    if res is not None:
        in_specs.append(pl.BlockSpec((tm, tn), lambda i, j: (i, j)))
        args.append(res)
    if head_dim:
        assert tn % head_dim == 0 and tn // head_dim <= LANES
        nh = tn // head_dim
        ind = np.zeros((tn, LANES), np.float32)
        ind[np.arange(tn), np.arange(tn) // head_dim] = 1.0
        gain_row = jnp.tile(head_gain.astype(F32), nh).reshape(1, tn)
        in_specs += [pl.BlockSpec((1, tn), lambda i, j: (0, 0)),
                     pl.BlockSpec((tn, LANES), lambda i, j: (0, 0)),
                     pl.BlockSpec((LANES, tn), lambda i, j: (0, 0))]
        args += [gain_row, jnp.asarray(ind, BF16), jnp.asarray(ind.T, BF16)]
    if head_major:
        assert tn % head_major == 0
        out_spec = pl.BlockSpec((tn // head_major, tm, head_major), lambda i, j: (j, i, 0))
        out_shape = jax.ShapeDtypeStruct((n_cols // head_major, n, head_major), out_dtype)
    else:
        out_spec = pl.BlockSpec((tm, tn), lambda i, j: (i, j))
        out_shape = jax.ShapeDtypeStruct((n, n_cols), out_dtype)
    return pl.pallas_call(
        functools.partial(_mm_body, has_res=res is not None, head_dim=head_dim, scale=scale,
                          head_major=head_major),
        grid=(n // tm, n_cols // tn),
        in_specs=in_specs,
        out_specs=out_spec,
        out_shape=out_shape,
        compiler_params=_cparams(("parallel", "arbitrary")),
        name="matmul",
    )(*args)


def _gla_tables(c):
    levels = []
    l = c // 2
    while l >= 1:
        levels.append(l)
        l //= 2
    rows = np.arange(c)
    wcum = (rows[None, :] <= rows[:, None]).astype(np.float32)
    wlev = []
    masks = [np.eye(c, dtype=np.float32)]
    for l in levels:
        blk = rows // l
        odd = (blk % 2) == 1
        ref = np.where(odd, blk * l, (blk + 1) * l)
        lo = np.minimum(rows, ref)
        hi = np.maximum(rows, ref)
        wlev.append(((rows[None, :] > lo[:, None]) & (rows[None, :] <= hi[:, None])).astype(np.float32))
        masks.append((odd[:, None] & (blk[None, :] == blk[:, None] - 1)).astype(np.float32))
    return levels, wcum, np.concatenate(wlev, axis=0), np.stack(masks, axis=0)


def _gla_body(q_ref, k_ref, v_ref, r_ref, z_ref, wup_ref, bd_ref, gout_ref, wcum_ref, wlev_ref, mask_ref,
              o_ref, st_ref, e_ref, x_ref, qe_ref, kr_ref, acc_ref, upd_ref, *, c, levels, scale, tau):
    tb, dk = q_ref.shape
    nch = tb // c
    nlev = len(levels)

    @pl.when(pl.program_id(2) == 0)
    def _():
        st_ref[...] = jnp.zeros_like(st_ref)

    zh, zm, _ = _split3(z_ref[...])
    wh, wm = wup_ref[0], wup_ref[1]
    pre = _dot(zh, wh) + _dot(zh, wm) + _dot(zm, wh) + bd_ref[...]
    g = (jnp.minimum(pre, 0.0) - jnp.log(1.0 + jnp.exp(-jnp.abs(pre)))) * (1.0 / tau)
    g_hi = g.astype(BF16)
    g_lo = (g - g_hi.astype(F32)).astype(BF16)
    for ci in range(nch):
        rows = slice(ci * c, (ci + 1) * c)
        ec = _dot(wcum_ref[...], jnp.concatenate([g_hi[rows], g_lo[rows]], axis=1))
        e_ref[0, rows, :] = ec[:, :dk] + ec[:, dk:]
        el = _dot(wlev_ref[...], g_hi[rows])
        for li in range(nlev):
            e_ref[li + 1, rows, :] = el[li * c:(li + 1) * c]
    q = q_ref[...].astype(F32) * scale
    k = k_ref[...].astype(F32)
    ecum = e_ref[0]
    qe_ref[...] = (q * jnp.exp(ecum)).astype(BF16)
    lasts = [ecum[(ci + 1) * c - 1:(ci + 1) * c, :] for ci in range(nch)]
    last_b = jnp.concatenate([jnp.broadcast_to(l, (c, dk)) for l in lasts], axis=0)
    kr_ref[...] = (k * jnp.exp(last_b - ecum)).astype(BF16)
    row = lax.broadcasted_iota(jnp.int32, (tb, dk), 0)
    for li, l in enumerate(levels):
        odd = ((row >> int(math.log2(l))) & 1) == 1
        x_ref[li] = (jnp.where(odd, q, k) * jnp.exp(e_ref[li + 1])).astype(BF16)
    qb = q.astype(BF16)
    kb = k.astype(BF16)
    for ci in range(nch):
        rows = slice(ci * c, (ci + 1) * c)
        a = mask_ref[0] * _dot_nt(qb[rows], kb[rows])
        for li in range(nlev):
            xc = x_ref[li, rows, :]
            a = a + mask_ref[li + 1] * _dot_nt(xc, xc)
        v = v_ref[rows, :]
        acc_ref[rows, :] = _dot(a.astype(BF16), v)
        upd_ref[ci] = _dot_tn(v, kr_ref[rows, :])
    st = st_ref[...]
    for ci in range(nch):
        rows = slice(ci * c, (ci + 1) * c)
        acc_ref[rows, :] = acc_ref[rows, :] + _dot_nt(qe_ref[rows, :], st.astype(BF16))
        st = st * jnp.exp(lasts[ci]) + upd_ref[ci]
    st_ref[...] = st
    o = acc_ref[...]
    o = o * lax.rsqrt(jnp.mean(o * o, axis=-1, keepdims=True) + EPS) * gout_ref[...]
    r = r_ref[...].astype(F32)
    o_ref[...] = (o * (r / (1.0 + jnp.exp(-r)))).astype(o_ref.dtype)


def gla_core(proj, z, w_up, b_decay, g_out, *, batch, heads, dk, dv, tb=1024, c=GLA_CHUNK):
    n = proj.shape[0]
    t = n // batch
    nb = t // tb
    levels, wcum, wlev, masks = _gla_tables(c)
    nlev = len(levels)
    rank = w_up.shape[0]
    wup_pad = jnp.zeros((LANES, heads * dk), F32).at[:rank].set(w_up.astype(F32))
    wup2 = jnp.stack(_split3(wup_pad)[:2], axis=0)
    kv0 = 2 * heads * dk // dv
    row_map = lambda b, h, i: b * nb + i
    const2 = lambda b, h, i: (0, 0)
    return pl.pallas_call(
        functools.partial(_gla_body, c=c, levels=levels, scale=dk ** -0.5, tau=GLA_GATE_TAU),
        grid=(batch, heads, nb),
        in_specs=[
            pl.BlockSpec((tb, dk), lambda b, h, i: (row_map(b, h, i), h)),
            pl.BlockSpec((tb, dk), lambda b, h, i: (row_map(b, h, i), heads + h)),
            pl.BlockSpec((tb, dv), lambda b, h, i: (row_map(b, h, i), kv0 + h)),
            pl.BlockSpec((tb, dv), lambda b, h, i: (row_map(b, h, i), kv0 + heads + h)),
            pl.BlockSpec((tb, LANES), lambda b, h, i: (row_map(b, h, i), 0)),
            pl.BlockSpec((2, LANES, dk), lambda b, h, i: (0, 0, h)),
            pl.BlockSpec((1, dk), lambda b, h, i: (0, h)),
            pl.BlockSpec((1, dv), const2),
            pl.BlockSpec(wcum.shape, const2),
            pl.BlockSpec(wlev.shape, const2),
            pl.BlockSpec(masks.shape, lambda b, h, i: (0, 0, 0)),
        ],
        out_specs=pl.BlockSpec((tb, dv), lambda b, h, i: (row_map(b, h, i), h)),
        out_shape=jax.ShapeDtypeStruct((n, heads * dv), BF16),
        scratch_shapes=[pltpu.VMEM((dv, dk), F32),
                        pltpu.VMEM((nlev + 1, tb, dk), F32),
                        pltpu.VMEM((nlev, tb, dk), BF16),
                        pltpu.VMEM((tb, dk), BF16),
                        pltpu.VMEM((tb, dk), BF16),
                        pltpu.VMEM((tb, dv), F32),
                        pltpu.VMEM((tb // c, dv, dk), F32)],
        compiler_params=_cparams(("parallel", "parallel", "arbitrary")),
        name="gla_core",
    )(proj, proj, proj, proj, z, wup2, b_decay.reshape(1, -1).astype(F32), g_out.reshape(1, -1).astype(F32),
      jnp.asarray(wcum, BF16), jnp.asarray(wlev, BF16), jnp.asarray(masks, F32))


def _swa_body(q_ref, kc_ref, kp_ref, vc_ref, vp_ref, sink_ref, bias_ref, qg_ref, o_ref, *, group, blk):
    kvh, _, hd = kc_ref.shape
    gw = group * hd
    ones = jnp.ones((SUBLANES, hd), BF16)
    for h in range(kvh):
        qh = q_ref[:, h * gw:(h + 1) * gw]
        q_all = jnp.concatenate([qh[:, g * hd:(g + 1) * hd] for g in range(group)], axis=0)
        qf = q_all.astype(F32)
        ssq = _dot_nt(ones, (qf * qf).astype(BF16))[0:1]
        inv_q = lax.rsqrt(ssq * (1.0 / hd) + EPS)
        k2 = jnp.concatenate([kp_ref[h], kc_ref[h]], axis=0)
        v2 = jnp.concatenate([vp_ref[h], vc_ref[h]], axis=0)
        k2g = (k2.astype(F32) * qg_ref[...]).astype(BF16)
        s = _dot_nt(k2g, q_all) * inv_q + bias_ref[...]
        sink = sink_ref[h]
        m = jnp.maximum(jnp.max(s, axis=0, keepdims=True), sink)
        p = jnp.exp(s - m)
        denom = jnp.sum(p, axis=0, keepdims=True) + jnp.exp(sink - m)
        o_t = _dot_tn(v2, p.astype(BF16)) * (1.0 / denom)
        o = jnp.concatenate([o_t[:, g * blk:(g + 1) * blk].T for g in range(group)], axis=1)
        o_ref[:, h * gw:(h + 1) * gw] = o.astype(o_ref.dtype)


def swa_attention(q, k, v, sinks, q_gain, *, batch, blk=SWA_WINDOW):
    n, qw = q.shape
    kvh, _, hd = k.shape
    group = qw // hd // kvh
    nb = n // batch // blk
    ki = np.arange(2 * blk)[:, None]
    qi = np.tile(np.arange(blk), group)[None, :]
    dist = blk + qi - ki
    band = (dist >= 0) & (dist < blk)
    bias = np.stack([np.where(band & (ki >= blk), 0.0, NEG_BIG), np.where(band, 0.0, NEG_BIG)], axis=0)
    sink_rows = jnp.repeat(sinks.astype(F32).reshape(kvh, group), blk, axis=1).reshape(kvh, 1, group * blk)
    row = lambda b, j: b * nb + j
    prev = lambda b, j: (0, b * nb + jnp.maximum(j - 1, 0), 0)
    cur = lambda b, j: (0, row(b, j), 0)
    kv_c = pl.BlockSpec((kvh, blk, hd), cur)
    kv_p = pl.BlockSpec((kvh, blk, hd), prev)
    return pl.pallas_call(
        functools.partial(_swa_body, group=group, blk=blk),
        grid=(batch, nb),
        in_specs=[
            pl.BlockSpec((blk, qw), lambda b, j: (row(b, j), 0)),
            kv_c, kv_p, kv_c, kv_p,
            pl.BlockSpec((kvh, 1, group * blk), lambda b, j: (0, 0, 0)),
            pl.BlockSpec((None, 2 * blk, group * blk), lambda b, j: (jnp.minimum(j, 1), 0, 0)),
            pl.BlockSpec((1, hd), lambda b, j: (0, 0)),
        ],
        out_specs=pl.BlockSpec((blk, qw), lambda b, j: (row(b, j), 0)),
        out_shape=jax.ShapeDtypeStruct((n, qw), BF16),
        compiler_params=_cparams(("parallel", "arbitrary")),
        name="swa_attention",
    )(q, k, k, v, v, sink_rows, jnp.asarray(bias.astype(np.float32)), q_gain.astype(F32).reshape(1, hd))


def _pack_rows(y, o_ref):
    rows, d = y.shape
    ns = _slab_rows(d)
    for s in range(ns):
        lo = y[:, (2 * s) * LANES:(2 * s + 1) * LANES]
        hi = y[:, (2 * s + 1) * LANES:(2 * s + 2) * LANES]
        lo_bits = lax.bitcast_convert_type(lo.astype(BF16).astype(F32), U32)
        hi_bits = lax.bitcast_convert_type(hi.astype(BF16).astype(F32), U32)
        o_ref[pl.ds(s, rows, stride=ns), :] = (lo_bits >> 16) | (hi_bits & jnp.uint32(0xFFFF0000))


def _slab_rows(d):
    assert d % (2 * SUBLANES * LANES) == 0
    return d // (2 * LANES)


def _unpack_words(w):
    lo = lax.bitcast_convert_type(w << 16, F32)
    hi = lax.bitcast_convert_type(w & jnp.uint32(0xFFFF0000), F32)
    return lo, hi


def _unpack_rows(x_ref, s, ns):
    return _unpack_words(x_ref[pl.ds(s, x_ref.shape[0] // ns, stride=ns), :])


def _router_body(x_ref, g_ref, wr_ref, br_ref, tri_ref, xn_ref, meta_ref, wts_ref, cnt_ref, carry_ref,
                 *, groups, per_group):
    @pl.when(pl.program_id(0) == 0)
    def _():
        carry_ref[...] = jnp.zeros_like(carry_ref)

    x = x_ref[...]
    xn = x * lax.rsqrt(jnp.mean(x * x, axis=-1, keepdims=True) + EPS) * g_ref[...]
    _pack_rows(xn, xn_ref)
    xh, xm, _ = _split3(xn)
    both = _dot(xh, wr_ref[...])
    logits = both[:, :LANES] + both[:, LANES:] + _dot(xm, wr_ref[:, :LANES]) + br_ref[...]
    lane = lax.broadcasted_iota(jnp.int32, logits.shape, 1).astype(F32)
    big = jnp.float32(LANES)
    neg = jnp.float32(-jnp.inf)
    gl = jnp.where(lane < groups, logits, neg)
    gm = jnp.max(gl, axis=-1, keepdims=True)
    p_top = 1.0 / jnp.sum(jnp.exp(gl - gm), axis=-1, keepdims=True)
    g_idx = jnp.min(jnp.where(gl == gm, lane, big), axis=-1, keepdims=True)
    lo = groups + g_idx * per_group
    el = jnp.where((lane >= lo) & (lane < lo + per_group), logits, neg)
    v1 = jnp.max(el, axis=-1, keepdims=True)
    i1 = jnp.min(jnp.where(el == v1, lane, big), axis=-1, keepdims=True)
    el2 = jnp.where(lane == i1, neg, el)
    v2 = jnp.max(el2, axis=-1, keepdims=True)
    i2 = jnp.min(jnp.where(el2 == v2, lane, big), axis=-1, keepdims=True)
    e21 = jnp.exp(v2 - v1)
    w1 = p_top / (1.0 + e21)
    w2 = p_top * e21 / (1.0 + e21)
    sel = jnp.where((lane == i1) | (lane == i2), 1.0, 0.0)
    before = _dot(tri_ref[...], sel.astype(BF16)) + carry_ref[...]
    r1 = jnp.sum(jnp.where(lane == i1, before, 0.0), axis=-1, keepdims=True)
    r2 = jnp.sum(jnp.where(lane == i2, before, 0.0), axis=-1, keepdims=True)
    carry = carry_ref[...] + jnp.sum(sel, axis=0, keepdims=True)
    carry_ref[...] = carry
    cnt_ref[...] = jnp.broadcast_to(carry, cnt_ref.shape)
    meta = jnp.where(lane == 0, i1 - groups, jnp.where(lane == 1, i2 - groups,
                     jnp.where(lane == 2, r1, jnp.where(lane == 3, r2, 0.0))))
    meta_ref[...] = meta.astype(jnp.int32)
    wts_ref[...] = jnp.where(lane == 0, w1, jnp.where(lane == 1, w2, 0.0))


def router(h, gain, w_group, b_group, w_expert, b_expert, tm=256):
    n, d = h.shape
    ns = _slab_rows(d)
    groups = w_group.shape[1]
    experts = w_expert.shape[1]
    wr = jnp.zeros((d, LANES), F32).at[:, :groups].set(w_group).at[:, groups:groups + experts].set(w_expert)
    br = jnp.zeros((1, LANES), F32).at[0, :groups].set(b_group).at[0, groups:groups + experts].set(b_expert)
    wr2 = jnp.concatenate(_split3(wr)[:2], axis=1)
    tri = jnp.asarray(np.tril(np.ones((tm, tm), np.float32), -1), BF16)
    row = lambda i: (i, 0)
    xn, meta, wts, cnt = pl.pallas_call(
        functools.partial(_router_body, groups=groups, per_group=experts // groups),
        grid=(n // tm,),
        in_specs=[pl.BlockSpec((tm, d), row), pl.BlockSpec((1, d), lambda i: (0, 0)),
                  pl.BlockSpec((d, 2 * LANES), lambda i: (0, 0)), pl.BlockSpec((1, LANES), lambda i: (0, 0)),
                  pl.BlockSpec((tm, tm), lambda i: (0, 0))],
        out_specs=[pl.BlockSpec((tm * ns, LANES), row),
                   pl.BlockSpec((tm, LANES), row), pl.BlockSpec((tm, LANES), row),
                   pl.BlockSpec((SUBLANES, LANES), lambda i: (0, 0))],
        out_shape=[jax.ShapeDtypeStruct((n * ns, LANES), U32),
                   jax.ShapeDtypeStruct((n, LANES), jnp.int32),
                   jax.ShapeDtypeStruct((n, LANES), F32),
                   jax.ShapeDtypeStruct((SUBLANES, LANES), F32)],
        scratch_shapes=[pltpu.VMEM((1, LANES), F32)],
        compiler_params=_cparams(("arbitrary",)),
        name="moe_router",
    )(h, gain.reshape(1, d).astype(F32), wr2, br, tri)
    counts = cnt[0, groups:groups + experts].astype(jnp.int32)
    return xn, meta[:, 0:2], meta[:, 2:4], wts, counts


def _row_copy(src_ref, src_row, dst_ref, dst_row, sem, ns):
    src = src_ref.at[pl.ds(pl.multiple_of(src_row * ns, ns), ns)]
    dst = dst_ref.at[pl.ds(pl.multiple_of(dst_row * ns, ns), ns)]
    return pltpu.make_async_copy(src, dst, sem)


def _pad_bits(tm):
    return [1 << b for b in range(int(math.log2(tm)))]


def _dispatch_body(pad_start_ref, pad_count_ref, nu_ref, pos_ref, x_ref, o_hbm, zero_ref, sem, *, tm, tile,
                   n_experts, ns):
    t = pl.program_id(0)
    half = tile // 2

    @pl.when(t == 0)
    def _():
        zero_ref[...] = jnp.zeros_like(zero_ref)

        def zero_tile(i, c):
            cps = [pltpu.make_async_copy(
                zero_ref, o_hbm.at[pl.ds(pl.multiple_of((i * tile + k * half) * ns, ns), half * ns)], sem)
                for k in range(2)]
            for cp in cps:
                cp.start()
            for cp in cps:
                cp.wait()
            return c
        lax.fori_loop(nu_ref[0], o_hbm.shape[0] // (tile * ns), zero_tile, 0)
        for phase in ("start", "wait"):
            for e in range(n_experts):
                cnt = pad_count_ref[e]
                off = pad_start_ref[e]
                for bit in _pad_bits(tile):
                    dst0 = pl.multiple_of(off * ns, ns)
                    cp = pltpu.make_async_copy(zero_ref.at[pl.ds(0, bit * ns)],
                                               o_hbm.at[pl.ds(dst0, bit * ns)], sem)

                    @pl.when((cnt & bit) != 0)
                    def _():
                        cp.start() if phase == "start" else cp.wait()
                    off = off + (cnt & bit)

    def start(r, c):
        _row_copy(x_ref, r, o_hbm, pos_ref[0, 0, r], sem, ns).start(priority=0)
        _row_copy(x_ref, r, o_hbm, pos_ref[0, 0, tm + r], sem, ns).start(priority=1)
        return c
    lax.fori_loop(0, tm, start, 0)

    def wait(r, c):
        _row_copy(x_ref, 0, o_hbm, 0, sem, ns).wait()
        _row_copy(x_ref, 0, o_hbm, 0, sem, ns).wait()
        return c
    lax.fori_loop(0, tm, wait, 0)


def moe_dispatch(xn, pos, pad_start, pad_count, n_used, p_rows, tile, ns, tm=256):
    n = xn.shape[0] // ns
    nt = n // tm
    n_experts = pad_start.shape[0]
    pos_t = pos.reshape(nt, tm, 2).transpose(0, 2, 1).reshape(nt, 1, 2 * tm)
    return pl.pallas_call(
        functools.partial(_dispatch_body, tm=tm, tile=tile, n_experts=n_experts, ns=ns),
        grid_spec=pltpu.PrefetchScalarGridSpec(
            num_scalar_prefetch=3,
            grid=(nt,),
            in_specs=[pl.BlockSpec((1, 1, 2 * tm), lambda t, ps, pc, nu: (t, 0, 0), memory_space=pltpu.SMEM),
                      pl.BlockSpec((tm * ns, LANES), lambda t, ps, pc, nu: (t, 0))],
            out_specs=pl.BlockSpec(memory_space=pl.ANY),
            scratch_shapes=[pltpu.VMEM((tile // 2 * ns, LANES), xn.dtype), pltpu.SemaphoreType.DMA(())],
        ),
        out_shape=jax.ShapeDtypeStruct((p_rows * ns, LANES), xn.dtype),
        compiler_params=_cparams(("arbitrary",)),
        name="moe_dispatch",
    )(pad_start, pad_count, n_used, pos_t, xn)


def _moe_up_body(te_ref, nu_ref, x_ref, wg_ref, wu_ref, o_ref, wgb_ref, wub_ref):
    t = pl.program_id(1)
    changed = te_ref[t] != te_ref[jnp.maximum(t - 1, 0)]

    @pl.when((t < nu_ref[0]) & ((t == 0) | changed))
    def _():
        wgb_ref[...] = wg_ref[...].astype(BF16)
        wub_ref[...] = wu_ref[...].astype(BF16)

    @pl.when(t < nu_ref[0])
    def _():
        ns = _slab_rows(wgb_ref.shape[0])
        a = None
        u = None
        for s in range(ns):
            x = jnp.concatenate([xs.astype(BF16) for xs in _unpack_rows(x_ref, s, ns)], axis=1)
            rows = slice(2 * s * LANES, (2 * s + 2) * LANES)
            da = _dot(x, wgb_ref[rows, :])
            du = _dot(x, wub_ref[rows, :])
            a = da if a is None else a + da
            u = du if u is None else u + du
        o_ref[...] = (a / (1.0 + jnp.exp(-a)) * u).astype(o_ref.dtype)

    @pl.when(t >= nu_ref[0])
    def _():
        o_ref[...] = jnp.zeros_like(o_ref)


def moe_up(xs, tile_expert, n_used, w_gate_up, layer, tm, fc):
    d, f2 = w_gate_up.shape[2:]
    ns = _slab_rows(d)
    p = xs.shape[0] // ns
    f = f2 // 2
    fc = min(fc, f)
    assert f % fc == 0
    nc = f // fc
    tile = lambda t, nu: jnp.minimum(t, nu[0] - 1)
    return pl.pallas_call(
        _moe_up_body,
        grid_spec=pltpu.PrefetchScalarGridSpec(
            num_scalar_prefetch=2,
            grid=(nc, p // tm),
            in_specs=[pl.BlockSpec((tm * ns, LANES), lambda c, t, te, nu: (tile(t, nu), 0)),
                      pl.BlockSpec((None, None, d, fc), lambda c, t, te, nu: (layer, te[tile(t, nu)], 0, c)),
                      pl.BlockSpec((None, None, d, fc), lambda c, t, te, nu: (layer, te[tile(t, nu)], 0, nc + c))],
            out_specs=pl.BlockSpec((tm, fc), lambda c, t, te, nu: (t, c)),
            scratch_shapes=[pltpu.VMEM((d, fc), BF16), pltpu.VMEM((d, fc), BF16)],
        ),
        out_shape=jax.ShapeDtypeStruct((p, f), BF16),
        compiler_params=_cparams(("arbitrary", "arbitrary")),
        name="moe_up",
    )(tile_expert, n_used, xs, w_gate_up, w_gate_up)


def _moe_down_body(te_ref, nu_ref, h_ref, w_ref, o_ref, wb_ref):
    t = pl.program_id(0)
    changed = te_ref[t] != te_ref[jnp.maximum(t - 1, 0)]

    @pl.when((t < nu_ref[0]) & ((t == 0) | changed))
    def _():
        wb_ref[...] = w_ref[...].astype(BF16)

    @pl.when(t < nu_ref[0])
    def _():
        _pack_rows(_dot(h_ref[...], wb_ref[...]), o_ref)

    @pl.when(t >= nu_ref[0])
    def _():
        o_ref[...] = jnp.zeros_like(o_ref)


def moe_down(hs, tile_expert, n_used, w_down, layer, tm):
    p, f = hs.shape
    d = w_down.shape[3]
    tile = lambda t, nu: jnp.minimum(t, nu[0] - 1)
    return pl.pallas_call(
        _moe_down_body,
        grid_spec=pltpu.PrefetchScalarGridSpec(
            num_scalar_prefetch=2,
            grid=(p // tm,),
            in_specs=[pl.BlockSpec((tm, f), lambda t, te, nu: (tile(t, nu), 0)),
                      pl.BlockSpec((None, None, f, d), lambda t, te, nu: (layer, te[tile(t, nu)], 0, 0))],
            out_specs=pl.BlockSpec((tm * _slab_rows(d), LANES), lambda t, te, nu: (t, 0)),
            scratch_shapes=[pltpu.VMEM((f, d), BF16)],
        ),
        out_shape=jax.ShapeDtypeStruct((p * _slab_rows(d), LANES), U32),
        compiler_params=_cparams(("arbitrary",)),
        name="moe_down",
    )(tile_expert, n_used, hs, w_down)


def _combine_body(pos_ref, pos_next_ref, h_ref, w_ref, y_hbm, *rest, tm, n_norm):
    if n_norm:
        gain_ref, rest = rest[0], rest[1:]
    o_ref = rest[0]
    norm_refs = rest[1:1 + n_norm]
    buf_ref, sems = rest[1 + n_norm:]
    d = h_ref.shape[1]
    ns = _slab_rows(d)
    rb = 64
    t = pl.program_id(0)
    nt = pl.num_programs(0)

    def fetch(idx_ref, slot):
        def start(r, c):
            _row_copy(y_hbm, idx_ref[0, 0, r], buf_ref.at[slot, 0], r, sems.at[slot], ns).start(priority=0)
            _row_copy(y_hbm, idx_ref[0, 0, tm + r], buf_ref.at[slot, 1], r, sems.at[slot], ns).start(priority=1)
            return c
        lax.fori_loop(0, tm, start, 0)

    def step(slot):
        @pl.when(t == 0)
        def _():
            fetch(pos_ref, slot)

        @pl.when(t + 1 < nt)
        def _():
            fetch(pos_next_ref, 1 - slot)

        def wait(r, c):
            _row_copy(y_hbm, 0, buf_ref.at[slot, 0], r, sems.at[slot], ns).wait()
            _row_copy(y_hbm, 0, buf_ref.at[slot, 1], r, sems.at[slot], ns).wait()
            return c
        lax.fori_loop(0, tm, wait, 0)
        for b0 in range(0, tm, rb):
            rows = slice(b0, b0 + rb)
            w = w_ref[rows, :]
            w0 = jnp.broadcast_to(w[:, 0:1], (rb, LANES))
            w1 = jnp.broadcast_to(w[:, 1:2], (rb, LANES))
            ssq = jnp.zeros((rb, LANES), F32)
            for s in range(ns):
                a_lo, a_hi = _unpack_words(buf_ref[slot, 0, pl.ds(b0 * ns + s, rb, stride=ns), :])
                b_lo, b_hi = _unpack_words(buf_ref[slot, 1, pl.ds(b0 * ns + s, rb, stride=ns), :])
                c_lo = slice((2 * s) * LANES, (2 * s + 1) * LANES)
                c_hi = slice((2 * s + 1) * LANES, (2 * s + 2) * LANES)
                o_lo = h_ref[rows, c_lo] + w0 * a_lo + w1 * b_lo
                o_hi = h_ref[rows, c_hi] + w0 * a_hi + w1 * b_hi
                o_ref[rows, c_lo] = o_lo
                o_ref[rows, c_hi] = o_hi
                if n_norm:
                    ssq = ssq + o_lo * o_lo + o_hi * o_hi
            if n_norm:
                inv = lax.rsqrt(jnp.sum(ssq, axis=-1, keepdims=True) * (1.0 / d) + EPS)
                inv = jnp.broadcast_to(inv, (rb, LANES))
                for j in range(d // LANES):
                    cols = slice(j * LANES, (j + 1) * LANES)
                    y = o_ref[rows, cols] * inv
                    for i, n_ref in enumerate(norm_refs):
                        n_ref[rows, cols] = (y * gain_ref[i:i + 1, cols]).astype(n_ref.dtype)

    for slot in range(2):
        pl.when(t % 2 == slot)(functools.partial(step, slot))


def moe_combine(h, wts, ys, pos, norm_gains=None, tm=256):
    n, d = h.shape
    nt = n // tm
    n_norm = 0 if norm_gains is None else norm_gains.shape[0]
    pos_t = pos.reshape(nt, tm, 2).transpose(0, 2, 1).reshape(nt, 1, 2 * tm)
    row = lambda t: (t, 0)
    in_specs = [pl.BlockSpec((1, 1, 2 * tm), lambda t: (t, 0, 0), memory_space=pltpu.SMEM),
                pl.BlockSpec((1, 1, 2 * tm), lambda t: (jnp.minimum(t + 1, nt - 1), 0, 0),
                             memory_space=pltpu.SMEM),
                pl.BlockSpec((tm, d), row),
                pl.BlockSpec((tm, LANES), row),
                pl.BlockSpec(memory_space=pl.ANY)]
    args = [pos_t, pos_t, h, wts, ys]
    if n_norm:
        in_specs.append(pl.BlockSpec((n_norm, d), lambda t: (0, 0)))
        args.append(norm_gains.astype(F32))
    outs = pl.pallas_call(
        functools.partial(_combine_body, tm=tm, n_norm=n_norm),
        grid=(nt,),
        in_specs=in_specs,
        out_specs=[pl.BlockSpec((tm, d), row)] * (1 + n_norm),
        out_shape=[jax.ShapeDtypeStruct((n, d), F32)] + [jax.ShapeDtypeStruct((n, d), BF16)] * n_norm,
        scratch_shapes=[pltpu.VMEM((2, 2, tm * _slab_rows(d), LANES), ys.dtype),
                        pltpu.SemaphoreType.DMA((2,))],
        compiler_params=_cparams(("arbitrary",)),
        name="moe_combine",
    )(*args)
    return outs[0] if not n_norm else outs


def _routing_tables(ids, ranks, counts, tm):
    n = ids.shape[0]
    n_experts = counts.shape[0]
    padded = ((counts + tm - 1) // tm) * tm
    ends = jnp.cumsum(padded)
    offs = ends - padded
    experts = jnp.arange(n_experts, dtype=jnp.int32)
    pos = ranks + jnp.sum(jnp.where(ids[:, :, None] == experts[None, None, :], offs[None, None, :], 0), axis=-1)
    p_rows = ((2 * n + n_experts * (tm - 1)) // tm) * tm
    tile_start = jnp.arange(p_rows // tm, dtype=jnp.int32) * tm
    tile_expert = jnp.minimum(jnp.sum((ends[None, :] <= tile_start[:, None]).astype(jnp.int32), axis=1),
                              n_experts - 1)
    n_used = (ends[-1:] // tm).astype(jnp.int32)
    return pos.astype(jnp.int32), tile_expert, n_used, (offs + counts).astype(jnp.int32), \
        (padded - counts).astype(jnp.int32), p_rows


def hier_moe_block(h, gain, w_group, b_group, w_expert, b_expert, w_gate_up, w_down, layer, *,
                   tm=MOE_TILE, fc=512, norm_gains=None):
    xn, ids, ranks, wts, counts = router(h, gain, w_group, b_group, w_expert, b_expert)
    pos, tile_expert, n_used, pad_start, pad_count, p_rows = _routing_tables(ids, ranks, counts, tm)
    xs = moe_dispatch(xn, pos, pad_start, pad_count, n_used, p_rows, tm, _slab_rows(h.shape[1]))
    hs = moe_up(xs, tile_expert, n_used, w_gate_up, layer, tm, fc)
    ys = moe_down(hs, tile_expert, n_used, w_down, layer, tm)
    return moe_combine(h, wts, ys, pos, norm_gains)


def kernel(x, a_norm_g, a_w_in, a_w_decay_up, a_b_decay, a_out_norm_g, a_w_out, kv_norm_g, w_kv, k_norm_g,
           b_norm_g, b_w_q, b_q_norm_g, b_sinks, b_w_out, ffn_norm_g, moe_w_group, moe_b_group, moe_w_expert,
           moe_b_expert, moe_w_gate_up, moe_w_down):
    bsz, t, d = x.shape
    n = bsz * t
    h = x.reshape(n, d)
    tm_mm = 1024

    def moe(h, layer, norm_gains=None):
        return hier_moe_block(h, ffn_norm_g[layer], moe_w_group[layer], moe_b_group[layer],
                              moe_w_expert[layer], moe_b_expert[layer], moe_w_gate_up, moe_w_down, layer,
                              norm_gains=norm_gains)

    heads = GLA_HEADS
    dk_all = a_w_decay_up.shape[2]
    dk = dk_all // heads
    dv_all = a_w_out.shape[1]
    dv = dv_all // heads
    n_main = 2 * dk_all + 2 * dv_all
    w_in = a_w_in[0].astype(BF16)
    w_z = jnp.zeros((d, LANES), BF16).at[:, :GLA_GATE_RANK].set(a_w_in[0][:, n_main:].astype(BF16))
    xn, z = rmsnorm(h, a_norm_g[0:1], proj_w=w_z)
    proj = matmul(xn, w_in, n_cols=n_main, tm=tm_mm, tn=1024, out_dtype=BF16)
    gated = gla_core(proj, z, a_w_decay_up[0], a_b_decay[0], a_out_norm_g[0],
                     batch=bsz, heads=heads, dk=dk, dv=dv)
    h = matmul(gated, a_w_out[0].astype(BF16), n_cols=d, tm=tm_mm, tn=1024, res=h)
    h, xn_kv, xn_b = moe(h, 0, jnp.stack([kv_norm_g, b_norm_g[0]], axis=0))

    hd = SWA_HEAD_DIM
    kvh = SWA_KV_HEADS
    hq = b_w_q.shape[2] // hd
    w_kv_b = w_kv.astype(BF16)
    kw = kvh * hd
    k_sh = matmul(xn_kv, w_kv_b, n_cols=kw, tm=tm_mm, tn=kw, out_dtype=BF16, head_dim=hd, head_gain=k_norm_g,
                  head_major=hd)
    v_sh = matmul(xn_kv, w_kv_b, n_cols=kw, col_block_off=1, tm=tm_mm, tn=kw, out_dtype=BF16, head_major=hd)
    q = matmul(xn_b, b_w_q[0].astype(BF16), n_cols=hq * hd, tm=tm_mm, tn=1024, out_dtype=BF16)
    o = swa_attention(q, k_sh, v_sh, b_sinks[0], b_q_norm_g[0] * hd ** -0.5, batch=bsz)
    h = matmul(o, b_w_out[0].astype(BF16), n_cols=d, tm=tm_mm, tn=1024, res=h)
    h = moe(h, 1)
    return h.reshape(bsz, t, d)
```

```python
import functools
import math

import numpy as np
import jax
import jax.numpy as jnp
from jax import lax
from jax.experimental import pallas as pl
from jax.experimental.pallas import tpu as pltpu

F32 = jnp.float32
BF16 = jnp.bfloat16
U32 = jnp.uint32
EPS = 1e-6

GLA_HEADS = 8
GLA_GATE_RANK = 16
GLA_GATE_TAU = 16.0
SWA_HEAD_DIM = 64
SWA_KV_HEADS = 8
SWA_WINDOW = 128

LANES = 128
SUBLANES = 8
VMEM_LIMIT_BYTES = 56 * 1024 * 1024

GLA_CHUNK = 128
NEG_BIG = -1e30
MOE_TILE = 256


def _cparams(sem):
    return pltpu.CompilerParams(dimension_semantics=sem, vmem_limit_bytes=VMEM_LIMIT_BYTES)


def _split3(x):
    hi = x.astype(BF16)
    r1 = x - hi.astype(F32)
    mid = r1.astype(BF16)
    lo = (r1 - mid.astype(F32)).astype(BF16)
    return hi, mid, lo


def _dot(a, b):
    return jnp.dot(a, b, preferred_element_type=F32)


def _dot_nt(a, b):
    return lax.dot_general(a, b, (((1,), (1,)), ((), ())), preferred_element_type=F32)


def _dot_tn(a, b):
    return lax.dot_general(a, b, (((0,), (0,)), ((), ())), preferred_element_type=F32)


def _rmsnorm_body(x_ref, g_ref, *refs, n_out, has_proj):
    if has_proj:
        w_ref, refs = refs[0], refs[1:]
    x = x_ref[...]
    y = x * lax.rsqrt(jnp.mean(x * x, axis=-1, keepdims=True) + EPS)
    for i in range(n_out):
        yi = (y * g_ref[i:i + 1, :]).astype(refs[i].dtype)
        refs[i][...] = yi
        if has_proj and i == 0:
            refs[n_out][...] = _dot(yi, w_ref[...])


def rmsnorm(x, gains, out_dtype=BF16, tm=256, proj_w=None):
    n, d = x.shape
    k = gains.shape[0]
    row = lambda i: (i, 0)
    in_specs = [pl.BlockSpec((tm, d), row), pl.BlockSpec((k, d), lambda i: (0, 0))]
    out_specs = [pl.BlockSpec((tm, d), row) for _ in range(k)]
    out_shape = [jax.ShapeDtypeStruct((n, d), out_dtype) for _ in range(k)]
    args = [x, gains.astype(F32)]
    if proj_w is not None:
        m = proj_w.shape[1]
        in_specs.append(pl.BlockSpec((d, m), lambda i: (0, 0)))
        out_specs.append(pl.BlockSpec((tm, m), row))
        out_shape.append(jax.ShapeDtypeStruct((n, m), F32))
        args.append(proj_w)
    outs = pl.pallas_call(
        functools.partial(_rmsnorm_body, n_out=k, has_proj=proj_w is not None),
        grid=(n // tm,),
        in_specs=in_specs,
        out_specs=out_specs,
        out_shape=out_shape,
        compiler_params=_cparams(("parallel",)),
        name="rmsnorm",
    )(*args)
    return outs


def _mm_body(*refs, has_res, head_dim, scale, head_major):
    a_ref, w_ref = refs[0], refs[1]
    pos = 2
    res_ref = None
    if has_res:
        res_ref = refs[pos]
        pos += 1
    if head_dim:
        gain_ref, ind_ref, indt_ref = refs[pos:pos + 3]
        pos += 3
    o_ref = refs[pos]
    y = _dot(a_ref[...], w_ref[...])
    if head_dim:
        ssq = _dot((y * y).astype(BF16), ind_ref[...])
        inv = lax.rsqrt(ssq * (1.0 / head_dim) + EPS)
        hi, mid, _ = _split3(inv)
        it = indt_ref[...]
        y = y * (_dot(hi, it) + _dot(mid, it)) * (gain_ref[...] * scale)
    if has_res:
        y = y + res_ref[...]
    if head_major:
        hd = o_ref.shape[-1]
        for hh in range(o_ref.shape[0]):
            o_ref[hh] = y[:, hh * hd:(hh + 1) * hd].astype(o_ref.dtype)
    else:
        o_ref[...] = y.astype(o_ref.dtype)


def matmul(a, w, *, n_cols, col_block_off=0, tm, tn, res=None, out_dtype=F32,
           head_dim=0, head_gain=None, scale=1.0, head_major=0):
    n, k = a.shape
    assert n % tm == 0 and n_cols % tn == 0
    in_specs = [pl.BlockSpec((tm, k), lambda i, j: (i, 0)),
                pl.BlockSpec((k, tn), lambda i, j: (0, j + col_block_off))]
    args = [a, w]
    if res is not None:
        in_specs.append(pl.BlockSpec((tm, tn), lambda i, j: (i, j)))
        args.append(res)
    if head_dim:
        assert tn % head_dim == 0 and tn // head_dim <= LANES
        nh = tn // head_dim
        ind = np.zeros((tn, LANES), np.float32)
        ind[np.arange(tn), np.arange(tn) // head_dim] = 1.0
        gain_row = jnp.tile(head_gain.astype(F32), nh).reshape(1, tn)
        in_specs += [pl.BlockSpec((1, tn), lambda i, j: (0, 0)),
                     pl.BlockSpec((tn, LANES), lambda i, j: (0, 0)),
                     pl.BlockSpec((LANES, tn), lambda i, j: (0, 0))]
        args += [gain_row, jnp.asarray(ind, BF16), jnp.asarray(ind.T, BF16)]
    if head_major:
        assert tn % head_major == 0
        out_spec = pl.BlockSpec((tn // head_major, tm, head_major), lambda i, j: (j, i, 0))
        out_shape = jax.ShapeDtypeStruct((n_cols // head_major, n, head_major), out_dtype)
    else:
        out_spec = pl.BlockSpec((tm, tn), lambda i, j: (i, j))
        out_shape = jax.ShapeDtypeStruct((n, n_cols), out_dtype)
    return pl.pallas_call(
        functools.partial(_mm_body, has_res=res is not None, head_dim=head_dim, scale=scale,
                          head_major=head_major),
        grid=(n // tm, n_cols // tn),
        in_specs=in_specs,
        out_specs=out_spec,
        out_shape=out_shape,
        compiler_params=_cparams(("parallel", "arbitrary")),
        name="matmul",
    )(*args)


def _gla_tables(c):
    levels = []
    l = c // 2
    while l >= 1:
        levels.append(l)
        l //= 2
    rows = np.arange(c)
    wcum = (rows[None, :] <= rows[:, None]).astype(np.float32)
    wlev = []
    masks = [np.eye(c, dtype=np.float32)]
    for l in levels:
        blk = rows // l
        odd = (blk % 2) == 1
        ref = np.where(odd, blk * l, (blk + 1) * l)
        lo = np.minimum(rows, ref)
        hi = np.maximum(rows, ref)
        wlev.append(((rows[None, :] > lo[:, None]) & (rows[None, :] <= hi[:, None])).astype(np.float32))
        masks.append((odd[:, None] & (blk[None, :] == blk[:, None] - 1)).astype(np.float32))
    return levels, wcum, np.concatenate(wlev, axis=0), np.stack(masks, axis=0)


def _gla_body(q_ref, k_ref, v_ref, r_ref, z_ref, wup_ref, bd_ref, gout_ref, wcum_ref, wlev_ref, mask_ref,
              o_ref, st_ref, e_ref, x_ref, qe_ref, kr_ref, acc_ref, upd_ref, *, c, levels, scale, tau):
    tb, dk = q_ref.shape
    nch = tb // c
    nlev = len(levels)

    @pl.when(pl.program_id(2) == 0)
    def _():
        st_ref[...] = jnp.zeros_like(st_ref)

    zh, zm, _ = _split3(z_ref[...])
    wh, wm = wup_ref[0], wup_ref[1]
    pre = _dot(zh, wh) + _dot(zh, wm) + _dot(zm, wh) + bd_ref[...]
    g = (jnp.minimum(pre, 0.0) - jnp.log(1.0 + jnp.exp(-jnp.abs(pre)))) * (1.0 / tau)
    g_hi = g.astype(BF16)
    g_lo = (g - g_hi.astype(F32)).astype(BF16)
    for ci in range(nch):
        rows = slice(ci * c, (ci + 1) * c)
        ec = _dot(wcum_ref[...], jnp.concatenate([g_hi[rows], g_lo[rows]], axis=1))
        e_ref[0, rows, :] = ec[:, :dk] + ec[:, dk:]
        el = _dot(wlev_ref[...], g_hi[rows])
        for li in range(nlev):
            e_ref[li + 1, rows, :] = el[li * c:(li + 1) * c]
    q = q_ref[...].astype(F32) * scale
    k = k_ref[...].astype(F32)
    ecum = e_ref[0]
    qe_ref[...] = (q * jnp.exp(ecum)).astype(BF16)
    lasts = [ecum[(ci + 1) * c - 1:(ci + 1) * c, :] for ci in range(nch)]
    last_b = jnp.concatenate([jnp.broadcast_to(l, (c, dk)) for l in lasts], axis=0)
    kr_ref[...] = (k * jnp.exp(last_b - ecum)).astype(BF16)
    row = lax.broadcasted_iota(jnp.int32, (tb, dk), 0)
    for li, l in enumerate(levels):
        odd = ((row >> int(math.log2(l))) & 1) == 1
        x_ref[li] = (jnp.where(odd, q, k) * jnp.exp(e_ref[li + 1])).astype(BF16)
    qb = q.astype(BF16)
    kb = k.astype(BF16)
    for ci in range(nch):
        rows = slice(ci * c, (ci + 1) * c)
        a = mask_ref[0] * _dot_nt(qb[rows], kb[rows])
        for li in range(nlev):
            xc = x_ref[li, rows, :]
            a = a + mask_ref[li + 1] * _dot_nt(xc, xc)
        v = v_ref[rows, :]
        acc_ref[rows, :] = _dot(a.astype(BF16), v)
        upd_ref[ci] = _dot_tn(v, kr_ref[rows, :])
    st = st_ref[...]
    for ci in range(nch):
        rows = slice(ci * c, (ci + 1) * c)
        acc_ref[rows, :] = acc_ref[rows, :] + _dot_nt(qe_ref[rows, :], st.astype(BF16))
        st = st * jnp.exp(lasts[ci]) + upd_ref[ci]
    st_ref[...] = st
    o = acc_ref[...]
    o = o * lax.rsqrt(jnp.mean(o * o, axis=-1, keepdims=True) + EPS) * gout_ref[...]
    r = r_ref[...].astype(F32)
    o_ref[...] = (o * (r / (1.0 + jnp.exp(-r)))).astype(o_ref.dtype)


def gla_core(proj, z, w_up, b_decay, g_out, *, batch, heads, dk, dv, tb=1024, c=GLA_CHUNK):
    n = proj.shape[0]
    t = n // batch
    nb = t // tb
    levels, wcum, wlev, masks = _gla_tables(c)
    nlev = len(levels)
    rank = w_up.shape[0]
    wup_pad = jnp.zeros((LANES, heads * dk), F32).at[:rank].set(w_up.astype(F32))
    wup2 = jnp.stack(_split3(wup_pad)[:2], axis=0)
    kv0 = 2 * heads * dk // dv
    row_map = lambda b, h, i: b * nb + i
    const2 = lambda b, h, i: (0, 0)
    return pl.pallas_call(
        functools.partial(_gla_body, c=c, levels=levels, scale=dk ** -0.5, tau=GLA_GATE_TAU),
        grid=(batch, heads, nb),
        in_specs=[
            pl.BlockSpec((tb, dk), lambda b, h, i: (row_map(b, h, i), h)),
            pl.BlockSpec((tb, dk), lambda b, h, i: (row_map(b, h, i), heads + h)),
            pl.BlockSpec((tb, dv), lambda b, h, i: (row_map(b, h, i), kv0 + h)),
            pl.BlockSpec((tb, dv), lambda b, h, i: (row_map(b, h, i), kv0 + heads + h)),
            pl.BlockSpec((tb, LANES), lambda b, h, i: (row_map(b, h, i), 0)),
            pl.BlockSpec((2, LANES, dk), lambda b, h, i: (0, 0, h)),
            pl.BlockSpec((1, dk), lambda b, h, i: (0, h)),
            pl.BlockSpec((1, dv), const2),
            pl.BlockSpec(wcum.shape, const2),
            pl.BlockSpec(wlev.shape, const2),
            pl.BlockSpec(masks.shape, lambda b, h, i: (0, 0, 0)),
        ],
        out_specs=pl.BlockSpec((tb, dv), lambda b, h, i: (row_map(b, h, i), h)),
        out_shape=jax.ShapeDtypeStruct((n, heads * dv), BF16),
        scratch_shapes=[pltpu.VMEM((dv, dk), F32),
                        pltpu.VMEM((nlev + 1, tb, dk), F32),
                        pltpu.VMEM((nlev, tb, dk), BF16),
                        pltpu.VMEM((tb, dk), BF16),
                        pltpu.VMEM((tb, dk), BF16),
                        pltpu.VMEM((tb, dv), F32),
                        pltpu.VMEM((tb // c, dv, dk), F32)],
        compiler_params=_cparams(("parallel", "parallel", "arbitrary")),
        name="gla_core",
    )(proj, proj, proj, proj, z, wup2, b_decay.reshape(1, -1).astype(F32), g_out.reshape(1, -1).astype(F32),
      jnp.asarray(wcum, BF16), jnp.asarray(wlev, BF16), jnp.asarray(masks, F32))


def _swa_body(q_ref, kc_ref, kp_ref, vc_ref, vp_ref, sink_ref, bias_ref, qg_ref, o_ref, *, group, blk):
    kvh, _, hd = kc_ref.shape
    gw = group * hd
    ones = jnp.ones((SUBLANES, hd), BF16)
    for h in range(kvh):
        qh = q_ref[:, h * gw:(h + 1) * gw]
        q_all = jnp.concatenate([qh[:, g * hd:(g + 1) * hd] for g in range(group)], axis=0)
        qf = q_all.astype(F32)
        ssq = _dot_nt(ones, (qf * qf).astype(BF16))[0:1]
        inv_q = lax.rsqrt(ssq * (1.0 / hd) + EPS)
        k2 = jnp.concatenate([kp_ref[h], kc_ref[h]], axis=0)
        v2 = jnp.concatenate([vp_ref[h], vc_ref[h]], axis=0)
        k2g = (k2.astype(F32) * qg_ref[...]).astype(BF16)
        s = _dot_nt(k2g, q_all) * inv_q + bias_ref[...]
        sink = sink_ref[h]
        m = jnp.maximum(jnp.max(s, axis=0, keepdims=True), sink)
        p = jnp.exp(s - m)
        denom = jnp.sum(p, axis=0, keepdims=True) + jnp.exp(sink - m)
        o_t = _dot_tn(v2, p.astype(BF16)) * (1.0 / denom)
        o = jnp.concatenate([o_t[:, g * blk:(g + 1) * blk].T for g in range(group)], axis=1)
        o_ref[:, h * gw:(h + 1) * gw] = o.astype(o_ref.dtype)


def swa_attention(q, k, v, sinks, q_gain, *, batch, blk=SWA_WINDOW):
    n, qw = q.shape
    kvh, _, hd = k.shape
    group = qw // hd // kvh
    nb = n // batch // blk
    ki = np.arange(2 * blk)[:, None]
    qi = np.tile(np.arange(blk), group)[None, :]
    dist = blk + qi - ki
    band = (dist >= 0) & (dist < blk)
    bias = np.stack([np.where(band & (ki >= blk), 0.0, NEG_BIG), np.where(band, 0.0, NEG_BIG)], axis=0)
    sink_rows = jnp.repeat(sinks.astype(F32).reshape(kvh, group), blk, axis=1).reshape(kvh, 1, group * blk)
    row = lambda b, j: b * nb + j
    prev = lambda b, j: (0, b * nb + jnp.maximum(j - 1, 0), 0)
    cur = lambda b, j: (0, row(b, j), 0)
    kv_c = pl.BlockSpec((kvh, blk, hd), cur)
    kv_p = pl.BlockSpec((kvh, blk, hd), prev)
    return pl.pallas_call(
        functools.partial(_swa_body, group=group, blk=blk),
        grid=(batch, nb),
        in_specs=[
            pl.BlockSpec((blk, qw), lambda b, j: (row(b, j), 0)),
            kv_c, kv_p, kv_c, kv_p,
            pl.BlockSpec((kvh, 1, group * blk), lambda b, j: (0, 0, 0)),
            pl.BlockSpec((None, 2 * blk, group * blk), lambda b, j: (jnp.minimum(j, 1), 0, 0)),
            pl.BlockSpec((1, hd), lambda b, j: (0, 0)),
        ],
        out_specs=pl.BlockSpec((blk, qw), lambda b, j: (row(b, j), 0)),
        out_shape=jax.ShapeDtypeStruct((n, qw), BF16),
        compiler_params=_cparams(("parallel", "arbitrary")),
        name="swa_attention",
    )(q, k, k, v, v, sink_rows, jnp.asarray(bias.astype(np.float32)), q_gain.astype(F32).reshape(1, hd))


def _pack_rows(y, o_ref):
    rows, d = y.shape
    ns = _slab_rows(d)
    for s in range(ns):
        lo = y[:, (2 * s) * LANES:(2 * s + 1) * LANES]
        hi = y[:, (2 * s + 1) * LANES:(2 * s + 2) * LANES]
        lo_bits = lax.bitcast_convert_type(lo.astype(BF16).astype(F32), U32)
        hi_bits = lax.bitcast_convert_type(hi.astype(BF16).astype(F32), U32)
        o_ref[pl.ds(s, rows, stride=ns), :] = (lo_bits >> 16) | (hi_bits & jnp.uint32(0xFFFF0000))


def _slab_rows(d):
    assert d % (2 * SUBLANES * LANES) == 0
    return d // (2 * LANES)


def _unpack_words(w):
    lo = lax.bitcast_convert_type(w << 16, F32)
    hi = lax.bitcast_convert_type(w & jnp.uint32(0xFFFF0000), F32)
    return lo, hi


def _unpack_rows(x_ref, s, ns):
    return _unpack_words(x_ref[pl.ds(s, x_ref.shape[0] // ns, stride=ns), :])


def _router_body(x_ref, g_ref, wr_ref, br_ref, tri_ref, xn_ref, meta_ref, wts_ref, cnt_ref, carry_ref,
                 *, groups, per_group):
    @pl.when(pl.program_id(0) == 0)
    def _():
        carry_ref[...] = jnp.zeros_like(carry_ref)

    x = x_ref[...]
    xn = x * lax.rsqrt(jnp.mean(x * x, axis=-1, keepdims=True) + EPS) * g_ref[...]
    _pack_rows(xn, xn_ref)
    xh, xm, _ = _split3(xn)
    both = _dot(xh, wr_ref[...])
    logits = both[:, :LANES] + both[:, LANES:] + _dot(xm, wr_ref[:, :LANES]) + br_ref[...]
    lane = lax.broadcasted_iota(jnp.int32, logits.shape, 1).astype(F32)
    big = jnp.float32(LANES)
    neg = jnp.float32(-jnp.inf)
    gl = jnp.where(lane < groups, logits, neg)
    gm = jnp.max(gl, axis=-1, keepdims=True)
    p_top = 1.0 / jnp.sum(jnp.exp(gl - gm), axis=-1, keepdims=True)
    g_idx = jnp.min(jnp.where(gl == gm, lane, big), axis=-1, keepdims=True)
    lo = groups + g_idx * per_group
    el = jnp.where((lane >= lo) & (lane < lo + per_group), logits, neg)
    v1 = jnp.max(el, axis=-1, keepdims=True)
    i1 = jnp.min(jnp.where(el == v1, lane, big), axis=-1, keepdims=True)
    el2 = jnp.where(lane == i1, neg, el)
    v2 = jnp.max(el2, axis=-1, keepdims=True)
    i2 = jnp.min(jnp.where(el2 == v2, lane, big), axis=-1, keepdims=True)
    e21 = jnp.exp(v2 - v1)
    w1 = p_top / (1.0 + e21)
    w2 = p_top * e21 / (1.0 + e21)
    sel = jnp.where((lane == i1) | (lane == i2), 1.0, 0.0)
    before = _dot(tri_ref[...], sel.astype(BF16)) + carry_ref[...]
    r1 = jnp.sum(jnp.where(lane == i1, before, 0.0), axis=-1, keepdims=True)
    r2 = jnp.sum(jnp.where(lane == i2, before, 0.0), axis=-1, keepdims=True)
    carry = carry_ref[...] + jnp.sum(sel, axis=0, keepdims=True)
    carry_ref[...] = carry
    cnt_ref[...] = jnp.broadcast_to(carry, cnt_ref.shape)
    meta = jnp.where(lane == 0, i1 - groups, jnp.where(lane == 1, i2 - groups,
                     jnp.where(lane == 2, r1, jnp.where(lane == 3, r2, 0.0))))
    meta_ref[...] = meta.astype(jnp.int32)
    wts_ref[...] = jnp.where(lane == 0, w1, jnp.where(lane == 1, w2, 0.0))


def router(h, gain, w_group, b_group, w_expert, b_expert, tm=256):
    n, d = h.shape
    ns = _slab_rows(d)
    groups = w_group.shape[1]
    experts = w_expert.shape[1]
    wr = jnp.zeros((d, LANES), F32).at[:, :groups].set(w_group).at[:, groups:groups + experts].set(w_expert)
    br = jnp.zeros((1, LANES), F32).at[0, :groups].set(b_group).at[0, groups:groups + experts].set(b_expert)
    wr2 = jnp.concatenate(_split3(wr)[:2], axis=1)
    tri = jnp.asarray(np.tril(np.ones((tm, tm), np.float32), -1), BF16)
    row = lambda i: (i, 0)
    xn, meta, wts, cnt = pl.pallas_call(
        functools.partial(_router_body, groups=groups, per_group=experts // groups),
        grid=(n // tm,),
        in_specs=[pl.BlockSpec((tm, d), row), pl.BlockSpec((1, d), lambda i: (0, 0)),
                  pl.BlockSpec((d, 2 * LANES), lambda i: (0, 0)), pl.BlockSpec((1, LANES), lambda i: (0, 0)),
                  pl.BlockSpec((tm, tm), lambda i: (0, 0))],
        out_specs=[pl.BlockSpec((tm * ns, LANES), row),
                   pl.BlockSpec((tm, LANES), row), pl.BlockSpec((tm, LANES), row),
                   pl.BlockSpec((SUBLANES, LANES), lambda i: (0, 0))],
        out_shape=[jax.ShapeDtypeStruct((n * ns, LANES), U32),
                   jax.ShapeDtypeStruct((n, LANES), jnp.int32),
                   jax.ShapeDtypeStruct((n, LANES), F32),
                   jax.ShapeDtypeStruct((SUBLANES, LANES), F32)],
        scratch_shapes=[pltpu.VMEM((1, LANES), F32)],
        compiler_params=_cparams(("arbitrary",)),
        name="moe_router",
    )(h, gain.reshape(1, d).astype(F32), wr2, br, tri)
    counts = cnt[0, groups:groups + experts].astype(jnp.int32)
    return xn, meta[:, 0:2], meta[:, 2:4], wts, counts


def _row_copy(src_ref, src_row, dst_ref, dst_row, sem, ns):
    src = src_ref.at[pl.ds(pl.multiple_of(src_row * ns, ns), ns)]
    dst = dst_ref.at[pl.ds(pl.multiple_of(dst_row * ns, ns), ns)]
    return pltpu.make_async_copy(src, dst, sem)


def _pad_bits(tm):
    return [1 << b for b in range(int(math.log2(tm)))]


def _dispatch_body(pad_start_ref, pad_count_ref, nu_ref, pos_ref, x_ref, o_hbm, zero_ref, sem, *, tm, tile,
                   n_experts, ns):
    t = pl.program_id(0)
    half = tile // 2

    @pl.when(t == 0)
    def _():
        zero_ref[...] = jnp.zeros_like(zero_ref)

        def zero_tile(i, c):
            cps = [pltpu.make_async_copy(
                zero_ref, o_hbm.at[pl.ds(pl.multiple_of((i * tile + k * half) * ns, ns), half * ns)], sem)
                for k in range(2)]
            for cp in cps:
                cp.start()
            for cp in cps:
                cp.wait()
            return c
        lax.fori_loop(nu_ref[0], o_hbm.shape[0] // (tile * ns), zero_tile, 0)
        for phase in ("start", "wait"):
            for e in range(n_experts):
                cnt = pad_count_ref[e]
                off = pad_start_ref[e]
                for bit in _pad_bits(tile):
                    dst0 = pl.multiple_of(off * ns, ns)
                    cp = pltpu.make_async_copy(zero_ref.at[pl.ds(0, bit * ns)],
                                               o_hbm.at[pl.ds(dst0, bit * ns)], sem)

                    @pl.when((cnt & bit) != 0)
                    def _():
                        cp.start() if phase == "start" else cp.wait()
                    off = off + (cnt & bit)

    def start(r, c):
        _row_copy(x_ref, r, o_hbm, pos_ref[0, 0, r], sem, ns).start(priority=0)
        _row_copy(x_ref, r, o_hbm, pos_ref[0, 0, tm + r], sem, ns).start(priority=1)
        return c
    lax.fori_loop(0, tm, start, 0)

    def wait(r, c):
        _row_copy(x_ref, 0, o_hbm, 0, sem, ns).wait()
        _row_copy(x_ref, 0, o_hbm, 0, sem, ns).wait()
        return c
    lax.fori_loop(0, tm, wait, 0)


def moe_dispatch(xn, pos, pad_start, pad_count, n_used, p_rows, tile, ns, tm=256):
    n = xn.shape[0] // ns
    nt = n // tm
    n_experts = pad_start.shape[0]
    pos_t = pos.reshape(nt, tm, 2).transpose(0, 2, 1).reshape(nt, 1, 2 * tm)
    return pl.pallas_call(
        functools.partial(_dispatch_body, tm=tm, tile=tile, n_experts=n_experts, ns=ns),
        grid_spec=pltpu.PrefetchScalarGridSpec(
            num_scalar_prefetch=3,
            grid=(nt,),
            in_specs=[pl.BlockSpec((1, 1, 2 * tm), lambda t, ps, pc, nu: (t, 0, 0), memory_space=pltpu.SMEM),
                      pl.BlockSpec((tm * ns, LANES), lambda t, ps, pc, nu: (t, 0))],
            out_specs=pl.BlockSpec(memory_space=pl.ANY),
            scratch_shapes=[pltpu.VMEM((tile // 2 * ns, LANES), xn.dtype), pltpu.SemaphoreType.DMA(())],
        ),
        out_shape=jax.ShapeDtypeStruct((p_rows * ns, LANES), xn.dtype),
        compiler_params=_cparams(("arbitrary",)),
        name="moe_dispatch",
    )(pad_start, pad_count, n_used, pos_t, xn)


def _moe_up_body(te_ref, nu_ref, x_ref, wg_ref, wu_ref, o_ref, wgb_ref, wub_ref):
    t = pl.program_id(1)
    changed = te_ref[t] != te_ref[jnp.maximum(t - 1, 0)]

    @pl.when((t < nu_ref[0]) & ((t == 0) | changed))
    def _():
        wgb_ref[...] = wg_ref[...].astype(BF16)
        wub_ref[...] = wu_ref[...].astype(BF16)

    @pl.when(t < nu_ref[0])
    def _():
        ns = _slab_rows(wgb_ref.shape[0])
        a = None
        u = None
        for s in range(ns):
            x = jnp.concatenate([xs.astype(BF16) for xs in _unpack_rows(x_ref, s, ns)], axis=1)
            rows = slice(2 * s * LANES, (2 * s + 2) * LANES)
            da = _dot(x, wgb_ref[rows, :])
            du = _dot(x, wub_ref[rows, :])
            a = da if a is None else a + da
            u = du if u is None else u + du
        o_ref[...] = (a / (1.0 + jnp.exp(-a)) * u).astype(o_ref.dtype)

    @pl.when(t >= nu_ref[0])
    def _():
        o_ref[...] = jnp.zeros_like(o_ref)


def moe_up(xs, tile_expert, n_used, w_gate_up, layer, tm, fc):
    d, f2 = w_gate_up.shape[2:]
    ns = _slab_rows(d)
    p = xs.shape[0] // ns
    f = f2 // 2
    fc = min(fc, f)
    assert f % fc == 0
    nc = f // fc
    tile = lambda t, nu: jnp.minimum(t, nu[0] - 1)
    return pl.pallas_call(
        _moe_up_body,
        grid_spec=pltpu.PrefetchScalarGridSpec(
            num_scalar_prefetch=2,
            grid=(nc, p // tm),
            in_specs=[pl.BlockSpec((tm * ns, LANES), lambda c, t, te, nu: (tile(t, nu), 0)),
                      pl.BlockSpec((None, None, d, fc), lambda c, t, te, nu: (layer, te[tile(t, nu)], 0, c)),
                      pl.BlockSpec((None, None, d, fc), lambda c, t, te, nu: (layer, te[tile(t, nu)], 0, nc + c))],
            out_specs=pl.BlockSpec((tm, fc), lambda c, t, te, nu: (t, c)),
            scratch_shapes=[pltpu.VMEM((d, fc), BF16), pltpu.VMEM((d, fc), BF16)],
        ),
        out_shape=jax.ShapeDtypeStruct((p, f), BF16),
        compiler_params=_cparams(("arbitrary", "arbitrary")),
        name="moe_up",
    )(tile_expert, n_used, xs, w_gate_up, w_gate_up)


def _moe_down_body(te_ref, nu_ref, h_ref, w_ref, o_ref, wb_ref):
    t = pl.program_id(0)
    changed = te_ref[t] != te_ref[jnp.maximum(t - 1, 0)]

    @pl.when((t < nu_ref[0]) & ((t == 0) | changed))
    def _():
        wb_ref[...] = w_ref[...].astype(BF16)

    @pl.when(t < nu_ref[0])
    def _():
        _pack_rows(_dot(h_ref[...], wb_ref[...]), o_ref)

    @pl.when(t >= nu_ref[0])
    def _():
        o_ref[...] = jnp.zeros_like(o_ref)


def moe_down(hs, tile_expert, n_used, w_down, layer, tm):
    p, f = hs.shape
    d = w_down.shape[3]
    tile = lambda t, nu: jnp.minimum(t, nu[0] - 1)
    return pl.pallas_call(
        _moe_down_body,
        grid_spec=pltpu.PrefetchScalarGridSpec(
            num_scalar_prefetch=2,
            grid=(p // tm,),
            in_specs=[pl.BlockSpec((tm, f), lambda t, te, nu: (tile(t, nu), 0)),
                      pl.BlockSpec((None, None, f, d), lambda t, te, nu: (layer, te[tile(t, nu)], 0, 0))],
            out_specs=pl.BlockSpec((tm * _slab_rows(d), LANES), lambda t, te, nu: (t, 0)),
            scratch_shapes=[pltpu.VMEM((f, d), BF16)],
        ),
        out_shape=jax.ShapeDtypeStruct((p * _slab_rows(d), LANES), U32),
        compiler_params=_cparams(("arbitrary",)),
        name="moe_down",
    )(tile_expert, n_used, hs, w_down)


def _combine_body(pos_ref, pos_next_ref, h_ref, w_ref, y_hbm, *rest, tm, n_norm):
    if n_norm:
        gain_ref, rest = rest[0], rest[1:]
    o_ref = rest[0]
    norm_refs = rest[1:1 + n_norm]
    buf_ref, sems = rest[1 + n_norm:]
    d = h_ref.shape[1]
    ns = _slab_rows(d)
    rb = 64
    t = pl.program_id(0)
    nt = pl.num_programs(0)

    def fetch(idx_ref, slot):
        def start(r, c):
            _row_copy(y_hbm, idx_ref[0, 0, r], buf_ref.at[slot, 0], r, sems.at[slot], ns).start(priority=0)
            _row_copy(y_hbm, idx_ref[0, 0, tm + r], buf_ref.at[slot, 1], r, sems.at[slot], ns).start(priority=1)
            return c
        lax.fori_loop(0, tm, start, 0)

    def step(slot):
        @pl.when(t == 0)
        def _():
            fetch(pos_ref, slot)

        @pl.when(t + 1 < nt)
        def _():
            fetch(pos_next_ref, 1 - slot)

        def wait(r, c):
            _row_copy(y_hbm, 0, buf_ref.at[slot, 0], r, sems.at[slot], ns).wait()
            _row_copy(y_hbm, 0, buf_ref.at[slot, 1], r, sems.at[slot], ns).wait()
            return c
        lax.fori_loop(0, tm, wait, 0)
        for b0 in range(0, tm, rb):
            rows = slice(b0, b0 + rb)
            w = w_ref[rows, :]
            w0 = jnp.broadcast_to(w[:, 0:1], (rb, LANES))
            w1 = jnp.broadcast_to(w[:, 1:2], (rb, LANES))
            ssq = jnp.zeros((rb, LANES), F32)
            for s in range(ns):
                a_lo, a_hi = _unpack_words(buf_ref[slot, 0, pl.ds(b0 * ns + s, rb, stride=ns), :])
                b_lo, b_hi = _unpack_words(buf_ref[slot, 1, pl.ds(b0 * ns + s, rb, stride=ns), :])
                c_lo = slice((2 * s) * LANES, (2 * s + 1) * LANES)
                c_hi = slice((2 * s + 1) * LANES, (2 * s + 2) * LANES)
                o_lo = h_ref[rows, c_lo] + w0 * a_lo + w1 * b_lo
                o_hi = h_ref[rows, c_hi] + w0 * a_hi + w1 * b_hi
                o_ref[rows, c_lo] = o_lo
                o_ref[rows, c_hi] = o_hi
                if n_norm:
                    ssq = ssq + o_lo * o_lo + o_hi * o_hi
            if n_norm:
                inv = lax.rsqrt(jnp.sum(ssq, axis=-1, keepdims=True) * (1.0 / d) + EPS)
                inv = jnp.broadcast_to(inv, (rb, LANES))
                for j in range(d // LANES):
                    cols = slice(j * LANES, (j + 1) * LANES)
                    y = o_ref[rows, cols] * inv
                    for i, n_ref in enumerate(norm_refs):
                        n_ref[rows, cols] = (y * gain_ref[i:i + 1, cols]).astype(n_ref.dtype)

    for slot in range(2):
        pl.when(t % 2 == slot)(functools.partial(step, slot))


def moe_combine(h, wts, ys, pos, norm_gains=None, tm=256):
    n, d = h.shape
    nt = n // tm
    n_norm = 0 if norm_gains is None else norm_gains.shape[0]
    pos_t = pos.reshape(nt, tm, 2).transpose(0, 2, 1).reshape(nt, 1, 2 * tm)
    row = lambda t: (t, 0)
    in_specs = [pl.BlockSpec((1, 1, 2 * tm), lambda t: (t, 0, 0), memory_space=pltpu.SMEM),
                pl.BlockSpec((1, 1, 2 * tm), lambda t: (jnp.minimum(t + 1, nt - 1), 0, 0),
                             memory_space=pltpu.SMEM),
                pl.BlockSpec((tm, d), row),
                pl.BlockSpec((tm, LANES), row),
                pl.BlockSpec(memory_space=pl.ANY)]
    args = [pos_t, pos_t, h, wts, ys]
    if n_norm:
        in_specs.append(pl.BlockSpec((n_norm, d), lambda t: (0, 0)))
        args.append(norm_gains.astype(F32))
    outs = pl.pallas_call(
        functools.partial(_combine_body, tm=tm, n_norm=n_norm),
        grid=(nt,),
        in_specs=in_specs,
        out_specs=[pl.BlockSpec((tm, d), row)] * (1 + n_norm),
        out_shape=[jax.ShapeDtypeStruct((n, d), F32)] + [jax.ShapeDtypeStruct((n, d), BF16)] * n_norm,
        scratch_shapes=[pltpu.VMEM((2, 2, tm * _slab_rows(d), LANES), ys.dtype),
                        pltpu.SemaphoreType.DMA((2,))],
        compiler_params=_cparams(("arbitrary",)),
        name="moe_combine",
    )(*args)
    return outs[0] if not n_norm else outs


def _routing_tables(ids, ranks, counts, tm):
    n = ids.shape[0]
    n_experts = counts.shape[0]
    padded = ((counts + tm - 1) // tm) * tm
    ends = jnp.cumsum(padded)
    offs = ends - padded
    experts = jnp.arange(n_experts, dtype=jnp.int32)
    pos = ranks + jnp.sum(jnp.where(ids[:, :, None] == experts[None, None, :], offs[None, None, :], 0), axis=-1)
    p_rows = ((2 * n + n_experts * (tm - 1)) // tm) * tm
    tile_start = jnp.arange(p_rows // tm, dtype=jnp.int32) * tm
    tile_expert = jnp.minimum(jnp.sum((ends[None, :] <= tile_start[:, None]).astype(jnp.int32), axis=1),
                              n_experts - 1)
    n_used = (ends[-1:] // tm).astype(jnp.int32)
    return pos.astype(jnp.int32), tile_expert, n_used, (offs + counts).astype(jnp.int32), \
        (padded - counts).astype(jnp.int32), p_rows


def hier_moe_block(h, gain, w_group, b_group, w_expert, b_expert, w_gate_up, w_down, layer, *,
                   tm=MOE_TILE, fc=512, norm_gains=None):
    xn, ids, ranks, wts, counts = router(h, gain, w_group, b_group, w_expert, b_expert)
    pos, tile_expert, n_used, pad_start, pad_count, p_rows = _routing_tables(ids, ranks, counts, tm)
    xs = moe_dispatch(xn, pos, pad_start, pad_count, n_used, p_rows, tm, _slab_rows(h.shape[1]))
    hs = moe_up(xs, tile_expert, n_used, w_gate_up, layer, tm, fc)
    ys = moe_down(hs, tile_expert, n_used, w_down, layer, tm)
    return moe_combine(h, wts, ys, pos, norm_gains)


def kernel(x, a_norm_g, a_w_in, a_w_decay_up, a_b_decay, a_out_norm_g, a_w_out, kv_norm_g, w_kv, k_norm_g,
           b_norm_g, b_w_q, b_q_norm_g, b_sinks, b_w_out, ffn_norm_g, moe_w_group, moe_b_group, moe_w_expert,
           moe_b_expert, moe_w_gate_up, moe_w_down):
    bsz, t, d = x.shape
    n = bsz * t
    h = x.reshape(n, d)
    tm_mm = 1024

    def moe(h, layer, norm_gains=None):
        return hier_moe_block(h, ffn_norm_g[layer], moe_w_group[layer], moe_b_group[layer],
                              moe_w_expert[layer], moe_b_expert[layer], moe_w_gate_up, moe_w_down, layer,
                              norm_gains=norm_gains)

    heads = GLA_HEADS
    dk_all = a_w_decay_up.shape[2]
    dk = dk_all // heads
    dv_all = a_w_out.shape[1]
    dv = dv_all // heads
    n_main = 2 * dk_all + 2 * dv_all
    w_in = a_w_in[0].astype(BF16)
    w_z = jnp.zeros((d, LANES), BF16).at[:, :GLA_GATE_RANK].set(a_w_in[0][:, n_main:].astype(BF16))
    xn, z = rmsnorm(h, a_norm_g[0:1], proj_w=w_z)
    proj = matmul(xn, w_in, n_cols=n_main, tm=tm_mm, tn=1024, out_dtype=BF16)
    gated = gla_core(proj, z, a_w_decay_up[0], a_b_decay[0], a_out_norm_g[0],
                     batch=bsz, heads=heads, dk=dk, dv=dv)
    h = matmul(gated, a_w_out[0].astype(BF16), n_cols=d, tm=tm_mm, tn=1024, res=h)
    h, xn_kv, xn_b = moe(h, 0, jnp.stack([kv_norm_g, b_norm_g[0]], axis=0))

    hd = SWA_HEAD_DIM
    kvh = SWA_KV_HEADS
    hq = b_w_q.shape[2] // hd
    w_kv_b = w_kv.astype(BF16)
    kw = kvh * hd
    k_sh = matmul(xn_kv, w_kv_b, n_cols=kw, tm=tm_mm, tn=kw, out_dtype=BF16, head_dim=hd, head_gain=k_norm_g,
                  head_major=hd)
    v_sh = matmul(xn_kv, w_kv_b, n_cols=kw, col_block_off=1, tm=tm_mm, tn=kw, out_dtype=BF16, head_major=hd)
    q = matmul(xn_b, b_w_q[0].astype(BF16), n_cols=hq * hd, tm=tm_mm, tn=1024, out_dtype=BF16)
    o = swa_attention(q, k_sh, v_sh, b_sinks[0], b_q_norm_g[0] * hd ** -0.5, batch=bsz)
    h = matmul(o, b_w_out[0].astype(BF16), n_cols=d, tm=tm_mm, tn=1024, res=h)
    h = moe(h, 1)
    return h.reshape(bsz, t, d)
```

```python
import functools
import math

import numpy as np
import jax
import jax.numpy as jnp
from jax import lax
from jax.experimental import pallas as pl
from jax.experimental.pallas import tpu as pltpu

F32 = jnp.float32
BF16 = jnp.bfloat16
U32 = jnp.uint32
EPS = 1e-6

GLA_HEADS = 8
GLA_GATE_RANK = 16
GLA_GATE_TAU = 16.0
SWA_HEAD_DIM = 64
SWA_KV_HEADS = 8
SWA_WINDOW = 128

LANES = 128
SUBLANES = 8
VMEM_LIMIT_BYTES = 56 * 1024 * 1024

GLA_CHUNK = 128
NEG_BIG = -1e30
MOE_TILE = 256


def _cparams(sem):
    return pltpu.CompilerParams(dimension_semantics=sem, vmem_limit_bytes=VMEM_LIMIT_BYTES)


def _split3(x):
    hi = x.astype(BF16)
    r1 = x - hi.astype(F32)
    mid = r1.astype(BF16)
    lo = (r1 - mid.astype(F32)).astype(BF16)
    return hi, mid, lo


def _dot(a, b):
    return jnp.dot(a, b, preferred_element_type=F32)


def _dot_nt(a, b):
    return lax.dot_general(a, b, (((1,), (1,)), ((), ())), preferred_element_type=F32)


def _dot_tn(a, b):
    return lax.dot_general(a, b, (((0,), (0,)), ((), ())), preferred_element_type=F32)


def _rmsnorm_body(x_ref, g_ref, *refs, n_out, has_proj):
    if has_proj:
        w_ref, refs = refs[0], refs[1:]
    x = x_ref[...]
    y = x * lax.rsqrt(jnp.mean(x * x, axis=-1, keepdims=True) + EPS)
    for i in range(n_out):
        yi = (y * g_ref[i:i + 1, :]).astype(refs[i].dtype)
        refs[i][...] = yi
        if has_proj and i == 0:
            refs[n_out][...] = _dot(yi, w_ref[...])


def rmsnorm(x, gains, out_dtype=BF16, tm=256, proj_w=None):
    n, d = x.shape
    k = gains.shape[0]
    row = lambda i: (i, 0)
    in_specs = [pl.BlockSpec((tm, d), row), pl.BlockSpec((k, d), lambda i: (0, 0))]
    out_specs = [pl.BlockSpec((tm, d), row) for _ in range(k)]
    out_shape = [jax.ShapeDtypeStruct((n, d), out_dtype) for _ in range(k)]
    args = [x, gains.astype(F32)]
    if proj_w is not None:
        m = proj_w.shape[1]
        in_specs.append(pl.BlockSpec((d, m), lambda i: (0, 0)))
        out_specs.append(pl.BlockSpec((tm, m), row))
        out_shape.append(jax.ShapeDtypeStruct((n, m), F32))
        args.append(proj_w)
    outs = pl.pallas_call(
        functools.partial(_rmsnorm_body, n_out=k, has_proj=proj_w is not None),
        grid=(n // tm,),
        in_specs=in_specs,
        out_specs=out_specs,
        out_shape=out_shape,
        compiler_params=_cparams(("parallel",)),
        name="rmsnorm",
    )(*args)
    return outs


def _mm_body(*refs, has_res, head_dim, scale, head_major):
    a_ref, w_ref = refs[0], refs[1]
    pos = 2
    res_ref = None
    if has_res:
        res_ref = refs[pos]
        pos += 1
    if head_dim:
        gain_ref, ind_ref, indt_ref = refs[pos:pos + 3]
        pos += 3
    o_ref = refs[pos]
    y = _dot(a_ref[...], w_ref[...])
    if head_dim:
        nc = ind_ref.shape[0]
        yk = y[:, :nc]
        ssq = _dot((yk * yk).astype(BF16), ind_ref[...])
        inv = lax.rsqrt(ssq * (1.0 / head_dim) + EPS)
        hi, mid, _ = _split3(inv)
        it = indt_ref[...]
        yk = yk * (_dot(hi, it) + _dot(mid, it)) * (gain_ref[...] * scale)
        y = yk if nc == y.shape[1] else jnp.concatenate([yk, y[:, nc:]], axis=1)
    if has_res:
        y = y + res_ref[...]
    if head_major:
        hd = o_ref.shape[-1]
        for hh in range(o_ref.shape[0]):
            o_ref[hh] = y[:, hh * hd:(hh + 1) * hd].astype(o_ref.dtype)
    else:
        o_ref[...] = y.astype(o_ref.dtype)


def matmul(a, w, *, n_cols, col_block_off=0, tm, tn, res=None, out_dtype=F32,
           head_dim=0, head_gain=None, scale=1.0, head_major=0, norm_cols=None):
    n, k = a.shape
    assert n % tm == 0 and n_cols % tn == 0
    in_specs = [pl.BlockSpec((tm, k), lambda i, j: (i, 0)),
                pl.BlockSpec((k, tn), lambda i, j: (0, j + col_block_off))]
    args = [a, w]
    if res is not None:
        in_specs.append(pl.BlockSpec((tm, tn), lambda i, j: (i, j)))
        args.append(res)
    if head_dim:
        nw = tn if norm_cols is None else norm_cols
        assert nw % head_dim == 0 and nw // head_dim <= LANES and nw <= tn
        nh = nw // head_dim
        ind = np.zeros((nw, LANES), np.float32)
        ind[np.arange(nw), np.arange(nw) // head_dim] = 1.0
        gain_row = jnp.tile(head_gain.astype(F32), nh).reshape(1, nw)
        in_specs += [pl.BlockSpec((1, nw), lambda i, j: (0, 0)),
                     pl.BlockSpec((nw, LANES), lambda i, j: (0, 0)),
                     pl.BlockSpec((LANES, nw), lambda i, j: (0, 0))]
        args += [gain_row, jnp.asarray(ind, BF16), jnp.asarray(ind.T, BF16)]
    if head_major:
        assert tn % head_major == 0
        out_spec = pl.BlockSpec((tn // head_major, tm, head_major), lambda i, j: (j, i, 0))
        out_shape = jax.ShapeDtypeStruct((n_cols // head_major, n, head_major), out_dtype)
    else:
        out_spec = pl.BlockSpec((tm, tn), lambda i, j: (i, j))
        out_shape = jax.ShapeDtypeStruct((n, n_cols), out_dtype)
    return pl.pallas_call(
        functools.partial(_mm_body, has_res=res is not None, head_dim=head_dim, scale=scale,
                          head_major=head_major),
        grid=(n // tm, n_cols // tn),
        in_specs=in_specs,
        out_specs=out_spec,
        out_shape=out_shape,
        compiler_params=_cparams(("parallel", "arbitrary")),
        name="matmul",
    )(*args)


def _gla_tables(c):
    levels = []
    l = c // 2
    while l >= 1:
        levels.append(l)
        l //= 2
    rows = np.arange(c)
    wcum = (rows[None, :] <= rows[:, None]).astype(np.float32)
    wlev = []
    masks = [np.eye(c, dtype=np.float32)]
    for l in levels:
        blk = rows // l
        odd = (blk % 2) == 1
        ref = np.where(odd, blk * l, (blk + 1) * l)
        lo = np.minimum(rows, ref)
        hi = np.maximum(rows, ref)
        wlev.append(((rows[None, :] > lo[:, None]) & (rows[None, :] <= hi[:, None])).astype(np.float32))
        masks.append((odd[:, None] & (blk[None, :] == blk[:, None] - 1)).astype(np.float32))
    return levels, wcum, np.concatenate(wlev, axis=0), np.stack(masks, axis=0)


def _gla_body(q_ref, k_ref, v_ref, r_ref, z_ref, wup_ref, bd_ref, gout_ref, wcum_ref, wlev_ref, mask_ref,
              o_ref, st_ref, e_ref, x_ref, qe_ref, kr_ref, acc_ref, upd_ref, *, c, levels, scale, tau):
    tb, dk = q_ref.shape
    nch = tb // c
    nlev = len(levels)

    @pl.when(pl.program_id(2) == 0)
    def _():
        st_ref[...] = jnp.zeros_like(st_ref)

    zh, zm, _ = _split3(z_ref[...])
    wh, wm = wup_ref[0], wup_ref[1]
    pre = _dot(zh, wh) + _dot(zh, wm) + _dot(zm, wh) + bd_ref[...]
    g = (jnp.minimum(pre, 0.0) - jnp.log(1.0 + jnp.exp(-jnp.abs(pre)))) * (1.0 / tau)
    g_hi = g.astype(BF16)
    g_lo = (g - g_hi.astype(F32)).astype(BF16)
    for ci in range(nch):
        rows = slice(ci * c, (ci + 1) * c)
        ec = _dot(wcum_ref[...], jnp.concatenate([g_hi[rows], g_lo[rows]], axis=1))
        e_ref[0, rows, :] = ec[:, :dk] + ec[:, dk:]
        el = _dot(wlev_ref[...], g_hi[rows])
        for li in range(nlev):
            e_ref[li + 1, rows, :] = el[li * c:(li + 1) * c]
    q = q_ref[...].astype(F32) * scale
    k = k_ref[...].astype(F32)
    ecum = e_ref[0]
    qe_ref[...] = (q * jnp.exp(ecum)).astype(BF16)
    lasts = [ecum[(ci + 1) * c - 1:(ci + 1) * c, :] for ci in range(nch)]
    last_b = jnp.concatenate([jnp.broadcast_to(l, (c, dk)) for l in lasts], axis=0)
    kr_ref[...] = (k * jnp.exp(last_b - ecum)).astype(BF16)
    row = lax.broadcasted_iota(jnp.int32, (tb, dk), 0)
    for li, l in enumerate(levels):
        odd = ((row >> int(math.log2(l))) & 1) == 1
        x_ref[li] = (jnp.where(odd, q, k) * jnp.exp(e_ref[li + 1])).astype(BF16)
    qb = q.astype(BF16)
    kb = k.astype(BF16)
    for ci in range(nch):
        rows = slice(ci * c, (ci + 1) * c)
        a = mask_ref[0] * _dot_nt(qb[rows], kb[rows])
        for li in range(nlev):
            xc = x_ref[li, rows, :]
            a = a + mask_ref[li + 1] * _dot_nt(xc, xc)
        v = v_ref[rows, :]
        acc_ref[rows, :] = _dot(a.astype(BF16), v)
        upd_ref[ci] = _dot_tn(v, kr_ref[rows, :])
    st = st_ref[...]
    for ci in range(nch):
        rows = slice(ci * c, (ci + 1) * c)
        acc_ref[rows, :] = acc_ref[rows, :] + _dot_nt(qe_ref[rows, :], st.astype(BF16))
        st = st * jnp.exp(lasts[ci]) + upd_ref[ci]
    st_ref[...] = st
    o = acc_ref[...]
    o = o * lax.rsqrt(jnp.mean(o * o, axis=-1, keepdims=True) + EPS) * gout_ref[...]
    r = r_ref[...].astype(F32)
    o_ref[...] = (o * (r / (1.0 + jnp.exp(-r)))).astype(o_ref.dtype)


def gla_core(proj, z, w_up, b_decay, g_out, *, batch, heads, dk, dv, tb=1024, c=GLA_CHUNK):
    n = proj.shape[0]
    t = n // batch
    nb = t // tb
    levels, wcum, wlev, masks = _gla_tables(c)
    nlev = len(levels)
    rank = w_up.shape[0]
    wup_pad = jnp.zeros((LANES, heads * dk), F32).at[:rank].set(w_up.astype(F32))
    wup2 = jnp.stack(_split3(wup_pad)[:2], axis=0)
    kv0 = 2 * heads * dk // dv
    row_map = lambda b, h, i: b * nb + i
    const2 = lambda b, h, i: (0, 0)
    return pl.pallas_call(
        functools.partial(_gla_body, c=c, levels=levels, scale=dk ** -0.5, tau=GLA_GATE_TAU),
        grid=(batch, heads, nb),
        in_specs=[
            pl.BlockSpec((tb, dk), lambda b, h, i: (row_map(b, h, i), h)),
            pl.BlockSpec((tb, dk), lambda b, h, i: (row_map(b, h, i), heads + h)),
            pl.BlockSpec((tb, dv), lambda b, h, i: (row_map(b, h, i), kv0 + h)),
            pl.BlockSpec((tb, dv), lambda b, h, i: (row_map(b, h, i), kv0 + heads + h)),
            pl.BlockSpec((tb, LANES), lambda b, h, i: (row_map(b, h, i), 0)),
            pl.BlockSpec((2, LANES, dk), lambda b, h, i: (0, 0, h)),
            pl.BlockSpec((1, dk), lambda b, h, i: (0, h)),
            pl.BlockSpec((1, dv), const2),
            pl.BlockSpec(wcum.shape, const2),
            pl.BlockSpec(wlev.shape, const2),
            pl.BlockSpec(masks.shape, lambda b, h, i: (0, 0, 0)),
        ],
        out_specs=pl.BlockSpec((tb, dv), lambda b, h, i: (row_map(b, h, i), h)),
        out_shape=jax.ShapeDtypeStruct((n, heads * dv), BF16),
        scratch_shapes=[pltpu.VMEM((dv, dk), F32),
                        pltpu.VMEM((nlev + 1, tb, dk), F32),
                        pltpu.VMEM((nlev, tb, dk), BF16),
                        pltpu.VMEM((tb, dk), BF16),
                        pltpu.VMEM((tb, dk), BF16),
                        pltpu.VMEM((tb, dv), F32),
                        pltpu.VMEM((tb // c, dv, dk), F32)],
        compiler_params=_cparams(("parallel", "parallel", "arbitrary")),
        name="gla_core",
    )(proj, proj, proj, proj, z, wup2, b_decay.reshape(1, -1).astype(F32), g_out.reshape(1, -1).astype(F32),
      jnp.asarray(wcum, BF16), jnp.asarray(wlev, BF16), jnp.asarray(masks, F32))


def _swa_body(q_ref, kc_ref, kp_ref, vc_ref, vp_ref, sink_ref, bias_ref, qg_ref, o_ref, *, group, blk):
    kvh, _, hd = kc_ref.shape
    gw = group * hd
    ones = jnp.ones((SUBLANES, hd), BF16)
    for h in range(kvh):
        qh = q_ref[:, h * gw:(h + 1) * gw]
        q_all = jnp.concatenate([qh[:, g * hd:(g + 1) * hd] for g in range(group)], axis=0)
        qf = q_all.astype(F32)
        ssq = _dot_nt(ones, (qf * qf).astype(BF16))[0:1]
        inv_q = lax.rsqrt(ssq * (1.0 / hd) + EPS)
        k2 = jnp.concatenate([kp_ref[h], kc_ref[h]], axis=0)
        v2 = jnp.concatenate([vp_ref[h], vc_ref[h]], axis=0)
        k2g = (k2.astype(F32) * qg_ref[...]).astype(BF16)
        s = _dot_nt(k2g, q_all) * inv_q + bias_ref[...]
        sink = sink_ref[h]
        m = jnp.maximum(jnp.max(s, axis=0, keepdims=True), sink)
        p = jnp.exp(s - m)
        denom = jnp.sum(p, axis=0, keepdims=True) + jnp.exp(sink - m)
        o_t = _dot_tn(v2, p.astype(BF16)) * (1.0 / denom)
        o = jnp.concatenate([o_t[:, g * blk:(g + 1) * blk].T for g in range(group)], axis=1)
        o_ref[:, h * gw:(h + 1) * gw] = o.astype(o_ref.dtype)


def swa_attention(q, k, v, sinks, q_gain, *, batch, blk=SWA_WINDOW):
    n, qw = q.shape
    kvh, _, hd = k.shape
    v_blk = 0
    if v is None:
        kvh, v, v_blk = kvh // 2, k, 1
    group = qw // hd // kvh
    nb = n // batch // blk
    ki = np.arange(2 * blk)[:, None]
    qi = np.tile(np.arange(blk), group)[None, :]
    dist = blk + qi - ki
    band = (dist >= 0) & (dist < blk)
    bias = np.stack([np.where(band & (ki >= blk), 0.0, NEG_BIG), np.where(band, 0.0, NEG_BIG)], axis=0)
    sink_rows = jnp.repeat(sinks.astype(F32).reshape(kvh, group), blk, axis=1).reshape(kvh, 1, group * blk)
    row = lambda b, j: b * nb + j
    prev = lambda b, j: (0, b * nb + jnp.maximum(j - 1, 0), 0)
    cur = lambda b, j: (0, row(b, j), 0)
    kv_c = pl.BlockSpec((kvh, blk, hd), cur)
    kv_p = pl.BlockSpec((kvh, blk, hd), prev)
    v_c = pl.BlockSpec((kvh, blk, hd), lambda b, j: (v_blk,) + cur(b, j)[1:])
    v_p = pl.BlockSpec((kvh, blk, hd), lambda b, j: (v_blk,) + prev(b, j)[1:])
    return pl.pallas_call(
        functools.partial(_swa_body, group=group, blk=blk),
        grid=(batch, nb),
        in_specs=[
            pl.BlockSpec((blk, qw), lambda b, j: (row(b, j), 0)),
            kv_c, kv_p, v_c, v_p,
            pl.BlockSpec((kvh, 1, group * blk), lambda b, j: (0, 0, 0)),
            pl.BlockSpec((None, 2 * blk, group * blk), lambda b, j: (jnp.minimum(j, 1), 0, 0)),
            pl.BlockSpec((1, hd), lambda b, j: (0, 0)),
        ],
        out_specs=pl.BlockSpec((blk, qw), lambda b, j: (row(b, j), 0)),
        out_shape=jax.ShapeDtypeStruct((n, qw), BF16),
        compiler_params=_cparams(("parallel", "arbitrary")),
        name="swa_attention",
    )(q, k, k, v, v, sink_rows, jnp.asarray(bias.astype(np.float32)), q_gain.astype(F32).reshape(1, hd))


def _pack_rows(y, o_ref):
    rows, d = y.shape
    ns = _slab_rows(d)
    for s in range(ns):
        lo = y[:, (2 * s) * LANES:(2 * s + 1) * LANES]
        hi = y[:, (2 * s + 1) * LANES:(2 * s + 2) * LANES]
        lo_bits = lax.bitcast_convert_type(lo.astype(BF16).astype(F32), U32)
        hi_bits = lax.bitcast_convert_type(hi.astype(BF16).astype(F32), U32)
        o_ref[pl.ds(s, rows, stride=ns), :] = (lo_bits >> 16) | (hi_bits & jnp.uint32(0xFFFF0000))


def _slab_rows(d):
    assert d % (2 * SUBLANES * LANES) == 0
    return d // (2 * LANES)


def _unpack_words(w):
    lo = lax.bitcast_convert_type(w << 16, F32)
    hi = lax.bitcast_convert_type(w & jnp.uint32(0xFFFF0000), F32)
    return lo, hi


def _unpack_rows(x_ref, s, ns):
    return _unpack_words(x_ref[pl.ds(s, x_ref.shape[0] // ns, stride=ns), :])


def _router_body(x_ref, g_ref, wr_ref, br_ref, tri_ref, xn_ref, meta_ref, wts_ref, cnt_ref, carry_ref,
                 *, groups, per_group):
    @pl.when(pl.program_id(0) == 0)
    def _():
        carry_ref[...] = jnp.zeros_like(carry_ref)

    x = x_ref[...]
    xn = x * lax.rsqrt(jnp.mean(x * x, axis=-1, keepdims=True) + EPS) * g_ref[...]
    _pack_rows(xn, xn_ref)
    xh, xm, _ = _split3(xn)
    both = _dot(xh, wr_ref[...])
    logits = both[:, :LANES] + both[:, LANES:] + _dot(xm, wr_ref[:, :LANES]) + br_ref[...]
    lane = lax.broadcasted_iota(jnp.int32, logits.shape, 1).astype(F32)
    big = jnp.float32(LANES)
    neg = jnp.float32(-jnp.inf)
    gl = jnp.where(lane < groups, logits, neg)
    gm = jnp.max(gl, axis=-1, keepdims=True)
    p_top = 1.0 / jnp.sum(jnp.exp(gl - gm), axis=-1, keepdims=True)
    g_idx = jnp.min(jnp.where(gl == gm, lane, big), axis=-1, keepdims=True)
    lo = groups + g_idx * per_group
    el = jnp.where((lane >= lo) & (lane < lo + per_group), logits, neg)
    v1 = jnp.max(el, axis=-1, keepdims=True)
    i1 = jnp.min(jnp.where(el == v1, lane, big), axis=-1, keepdims=True)
    el2 = jnp.where(lane == i1, neg, el)
    v2 = jnp.max(el2, axis=-1, keepdims=True)
    i2 = jnp.min(jnp.where(el2 == v2, lane, big), axis=-1, keepdims=True)
    e21 = jnp.exp(v2 - v1)
    w1 = p_top / (1.0 + e21)
    w2 = p_top * e21 / (1.0 + e21)
    sel = jnp.where((lane == i1) | (lane == i2), 1.0, 0.0)
    before = _dot(tri_ref[...], sel.astype(BF16)) + carry_ref[...]
    r1 = jnp.sum(jnp.where(lane == i1, before, 0.0), axis=-1, keepdims=True)
    r2 = jnp.sum(jnp.where(lane == i2, before, 0.0), axis=-1, keepdims=True)
    carry = carry_ref[...] + jnp.sum(sel, axis=0, keepdims=True)
    carry_ref[...] = carry
    cnt_ref[...] = jnp.broadcast_to(carry, cnt_ref.shape)
    meta = jnp.where(lane == 0, i1 - groups, jnp.where(lane == 1, i2 - groups,
                     jnp.where(lane == 2, r1, jnp.where(lane == 3, r2, 0.0))))
    meta_ref[...] = meta.astype(jnp.int32)
    wts_ref[...] = jnp.where(lane == 0, w1, jnp.where(lane == 1, w2, 0.0))


def router(h, gain, w_group, b_group, w_expert, b_expert, tm=256):
    n, d = h.shape
    ns = _slab_rows(d)
    groups = w_group.shape[1]
    experts = w_expert.shape[1]
    wr = jnp.zeros((d, LANES), F32).at[:, :groups].set(w_group).at[:, groups:groups + experts].set(w_expert)
    br = jnp.zeros((1, LANES), F32).at[0, :groups].set(b_group).at[0, groups:groups + experts].set(b_expert)
    wr2 = jnp.concatenate(_split3(wr)[:2], axis=1)
    tri = jnp.asarray(np.tril(np.ones((tm, tm), np.float32), -1), BF16)
    row = lambda i: (i, 0)
    xn, meta, wts, cnt = pl.pallas_call(
        functools.partial(_router_body, groups=groups, per_group=experts // groups),
        grid=(n // tm,),
        in_specs=[pl.BlockSpec((tm, d), row), pl.BlockSpec((1, d), lambda i: (0, 0)),
                  pl.BlockSpec((d, 2 * LANES), lambda i: (0, 0)), pl.BlockSpec((1, LANES), lambda i: (0, 0)),
                  pl.BlockSpec((tm, tm), lambda i: (0, 0))],
        out_specs=[pl.BlockSpec((tm * ns, LANES), row),
                   pl.BlockSpec((tm, LANES), row), pl.BlockSpec((tm, LANES), row),
                   pl.BlockSpec((SUBLANES, LANES), lambda i: (0, 0))],
        out_shape=[jax.ShapeDtypeStruct((n * ns, LANES), U32),
                   jax.ShapeDtypeStruct((n, LANES), jnp.int32),
                   jax.ShapeDtypeStruct((n, LANES), F32),
                   jax.ShapeDtypeStruct((SUBLANES, LANES), F32)],
        scratch_shapes=[pltpu.VMEM((1, LANES), F32)],
        compiler_params=_cparams(("arbitrary",)),
        name="moe_router",
    )(h, gain.reshape(1, d).astype(F32), wr2, br, tri)
    counts = cnt[0, groups:groups + experts].astype(jnp.int32)
    return xn, meta[:, 0:2], meta[:, 2:4], wts, counts


def _row_copy(src_ref, src_row, dst_ref, dst_row, sem, ns):
    src = src_ref.at[pl.ds(pl.multiple_of(src_row * ns, ns), ns)]
    dst = dst_ref.at[pl.ds(pl.multiple_of(dst_row * ns, ns), ns)]
    return pltpu.make_async_copy(src, dst, sem)


def _pad_bits(tm):
    return [1 << b for b in range(int(math.log2(tm)))]


def _dispatch_body(pad_start_ref, pad_count_ref, nu_ref, pos_ref, x_ref, o_hbm, zero_ref, sem, *, tm, tile,
                   n_experts, ns):
    t = pl.program_id(0)
    half = tile // 2

    @pl.when(t == 0)
    def _():
        zero_ref[...] = jnp.zeros_like(zero_ref)

        def zero_tile(i, c):
            cps = [pltpu.make_async_copy(
                zero_ref, o_hbm.at[pl.ds(pl.multiple_of((i * tile + k * half) * ns, ns), half * ns)], sem)
                for k in range(2)]
            for cp in cps:
                cp.start()
            for cp in cps:
                cp.wait()
            return c
        lax.fori_loop(nu_ref[0], o_hbm.shape[0] // (tile * ns), zero_tile, 0)
        for phase in ("start", "wait"):
            for e in range(n_experts):
                cnt = pad_count_ref[e]
                off = pad_start_ref[e]
                for bit in _pad_bits(tile):
                    dst0 = pl.multiple_of(off * ns, ns)
                    cp = pltpu.make_async_copy(zero_ref.at[pl.ds(0, bit * ns)],
                                               o_hbm.at[pl.ds(dst0, bit * ns)], sem)

                    @pl.when((cnt & bit) != 0)
                    def _():
                        cp.start() if phase == "start" else cp.wait()
                    off = off + (cnt & bit)

    def start(r, c):
        _row_copy(x_ref, r, o_hbm, pos_ref[0, 0, r], sem, ns).start(priority=0)
        _row_copy(x_ref, r, o_hbm, pos_ref[0, 0, tm + r], sem, ns).start(priority=1)
        return c
    lax.fori_loop(0, tm, start, 0)

    def wait(r, c):
        _row_copy(x_ref, 0, o_hbm, 0, sem, ns).wait()
        _row_copy(x_ref, 0, o_hbm, 0, sem, ns).wait()
        return c
    lax.fori_loop(0, tm, wait, 0)


def moe_dispatch(xn, pos, pad_start, pad_count, n_used, p_rows, tile, ns, tm=256):
    n = xn.shape[0] // ns
    nt = n // tm
    n_experts = pad_start.shape[0]
    pos_t = pos.reshape(nt, tm, 2).transpose(0, 2, 1).reshape(nt, 1, 2 * tm)
    return pl.pallas_call(
        functools.partial(_dispatch_body, tm=tm, tile=tile, n_experts=n_experts, ns=ns),
        grid_spec=pltpu.PrefetchScalarGridSpec(
            num_scalar_prefetch=3,
            grid=(nt,),
            in_specs=[pl.BlockSpec((1, 1, 2 * tm), lambda t, ps, pc, nu: (t, 0, 0), memory_space=pltpu.SMEM),
                      pl.BlockSpec((tm * ns, LANES), lambda t, ps, pc, nu: (t, 0))],
            out_specs=pl.BlockSpec(memory_space=pl.ANY),
            scratch_shapes=[pltpu.VMEM((tile // 2 * ns, LANES), xn.dtype), pltpu.SemaphoreType.DMA(())],
        ),
        out_shape=jax.ShapeDtypeStruct((p_rows * ns, LANES), xn.dtype),
        compiler_params=_cparams(("arbitrary",)),
        name="moe_dispatch",
    )(pad_start, pad_count, n_used, pos_t, xn)


def _moe_up_body(te_ref, nu_ref, x_ref, wg_ref, wu_ref, o_ref, wgb_ref, wub_ref):
    t = pl.program_id(1)
    changed = te_ref[t] != te_ref[jnp.maximum(t - 1, 0)]

    @pl.when((t < nu_ref[0]) & ((t == 0) | changed))
    def _():
        wgb_ref[...] = wg_ref[...].astype(BF16)
        wub_ref[...] = wu_ref[...].astype(BF16)

    @pl.when(t < nu_ref[0])
    def _():
        ns = _slab_rows(wgb_ref.shape[0])
        a = None
        u = None
        for s in range(ns):
            x = jnp.concatenate([xs.astype(BF16) for xs in _unpack_rows(x_ref, s, ns)], axis=1)
            rows = slice(2 * s * LANES, (2 * s + 2) * LANES)
            da = _dot(x, wgb_ref[rows, :])
            du = _dot(x, wub_ref[rows, :])
            a = da if a is None else a + da
            u = du if u is None else u + du
        o_ref[...] = (a / (1.0 + jnp.exp(-a)) * u).astype(o_ref.dtype)

    @pl.when(t >= nu_ref[0])
    def _():
        o_ref[...] = jnp.zeros_like(o_ref)


def moe_up(xs, tile_expert, n_used, w_gate_up, layer, tm, fc):
    d, f2 = w_gate_up.shape[2:]
    ns = _slab_rows(d)
    p = xs.shape[0] // ns
    f = f2 // 2
    fc = min(fc, f)
    assert f % fc == 0
    nc = f // fc
    tile = lambda t, nu: jnp.minimum(t, nu[0] - 1)
    return pl.pallas_call(
        _moe_up_body,
        grid_spec=pltpu.PrefetchScalarGridSpec(
            num_scalar_prefetch=2,
            grid=(nc, p // tm),
            in_specs=[pl.BlockSpec((tm * ns, LANES), lambda c, t, te, nu: (tile(t, nu), 0)),
                      pl.BlockSpec((None, None, d, fc), lambda c, t, te, nu: (layer, te[tile(t, nu)], 0, c)),
                      pl.BlockSpec((None, None, d, fc), lambda c, t, te, nu: (layer, te[tile(t, nu)], 0, nc + c))],
            out_specs=pl.BlockSpec((tm, fc), lambda c, t, te, nu: (t, c)),
            scratch_shapes=[pltpu.VMEM((d, fc), BF16), pltpu.VMEM((d, fc), BF16)],
        ),
        out_shape=jax.ShapeDtypeStruct((p, f), BF16),
        compiler_params=_cparams(("arbitrary", "arbitrary")),
        name="moe_up",
    )(tile_expert, n_used, xs, w_gate_up, w_gate_up)


def _moe_down_body(te_ref, nu_ref, h_ref, w_ref, o_ref, wb_ref):
    t = pl.program_id(0)
    changed = te_ref[t] != te_ref[jnp.maximum(t - 1, 0)]

    @pl.when((t < nu_ref[0]) & ((t == 0) | changed))
    def _():
        wb_ref[...] = w_ref[...].astype(BF16)

    @pl.when(t < nu_ref[0])
    def _():
        _pack_rows(_dot(h_ref[...], wb_ref[...]), o_ref)

    @pl.when(t >= nu_ref[0])
    def _():
        o_ref[...] = jnp.zeros_like(o_ref)


def moe_down(hs, tile_expert, n_used, w_down, layer, tm):
    p, f = hs.shape
    d = w_down.shape[3]
    tile = lambda t, nu: jnp.minimum(t, nu[0] - 1)
    return pl.pallas_call(
        _moe_down_body,
        grid_spec=pltpu.PrefetchScalarGridSpec(
            num_scalar_prefetch=2,
            grid=(p // tm,),
            in_specs=[pl.BlockSpec((tm, f), lambda t, te, nu: (tile(t, nu), 0)),
                      pl.BlockSpec((None, None, f, d), lambda t, te, nu: (layer, te[tile(t, nu)], 0, 0))],
            out_specs=pl.BlockSpec((tm * _slab_rows(d), LANES), lambda t, te, nu: (t, 0)),
            scratch_shapes=[pltpu.VMEM((f, d), BF16)],
        ),
        out_shape=jax.ShapeDtypeStruct((p * _slab_rows(d), LANES), U32),
        compiler_params=_cparams(("arbitrary",)),
        name="moe_down",
    )(tile_expert, n_used, hs, w_down)


def _combine_body(pos_ref, pos_next_ref, h_ref, w_ref, y_hbm, *rest, tm, n_norm):
    if n_norm:
        gain_ref, rest = rest[0], rest[1:]
    o_ref = rest[0]
    norm_refs = rest[1:1 + n_norm]
    buf_ref, sems = rest[1 + n_norm:]
    d = h_ref.shape[1]
    ns = _slab_rows(d)
    rb = 64
    t = pl.program_id(0)
    nt = pl.num_programs(0)

    def fetch(idx_ref, slot):
        def start(r, c):
            _row_copy(y_hbm, idx_ref[0, 0, r], buf_ref.at[slot, 0], r, sems.at[slot], ns).start(priority=0)
            _row_copy(y_hbm, idx_ref[0, 0, tm + r], buf_ref.at[slot, 1], r, sems.at[slot], ns).start(priority=1)
            return c
        lax.fori_loop(0, tm, start, 0)

    def step(slot):
        @pl.when(t == 0)
        def _():
            fetch(pos_ref, slot)

        @pl.when(t + 1 < nt)
        def _():
            fetch(pos_next_ref, 1 - slot)

        def wait(r, c):
            _row_copy(y_hbm, 0, buf_ref.at[slot, 0], r, sems.at[slot], ns).wait()
            _row_copy(y_hbm, 0, buf_ref.at[slot, 1], r, sems.at[slot], ns).wait()
            return c
        lax.fori_loop(0, tm, wait, 0)
        for b0 in range(0, tm, rb):
            rows = slice(b0, b0 + rb)
            w = w_ref[rows, :]
            w0 = jnp.broadcast_to(w[:, 0:1], (rb, LANES))
            w1 = jnp.broadcast_to(w[:, 1:2], (rb, LANES))
            ssq = jnp.zeros((rb, LANES), F32)
            for s in range(ns):
                a_lo, a_hi = _unpack_words(buf_ref[slot, 0, pl.ds(b0 * ns + s, rb, stride=ns), :])
                b_lo, b_hi = _unpack_words(buf_ref[slot, 1, pl.ds(b0 * ns + s, rb, stride=ns), :])
                c_lo = slice((2 * s) * LANES, (2 * s + 1) * LANES)
                c_hi = slice((2 * s + 1) * LANES, (2 * s + 2) * LANES)
                o_lo = h_ref[rows, c_lo] + w0 * a_lo + w1 * b_lo
                o_hi = h_ref[rows, c_hi] + w0 * a_hi + w1 * b_hi
                o_ref[rows, c_lo] = o_lo
                o_ref[rows, c_hi] = o_hi
                if n_norm:
                    ssq = ssq + o_lo * o_lo + o_hi * o_hi
            if n_norm:
                inv = lax.rsqrt(jnp.sum(ssq, axis=-1, keepdims=True) * (1.0 / d) + EPS)
                inv = jnp.broadcast_to(inv, (rb, LANES))
                for j in range(d // LANES):
                    cols = slice(j * LANES, (j + 1) * LANES)
                    y = o_ref[rows, cols] * inv
                    for i, n_ref in enumerate(norm_refs):
                        n_ref[rows, cols] = (y * gain_ref[i:i + 1, cols]).astype(n_ref.dtype)

    for slot in range(2):
        pl.when(t % 2 == slot)(functools.partial(step, slot))


def moe_combine(h, wts, ys, pos, norm_gains=None, tm=256):
    n, d = h.shape
    nt = n // tm
    n_norm = 0 if norm_gains is None else norm_gains.shape[0]
    pos_t = pos.reshape(nt, tm, 2).transpose(0, 2, 1).reshape(nt, 1, 2 * tm)
    row = lambda t: (t, 0)
    in_specs = [pl.BlockSpec((1, 1, 2 * tm), lambda t: (t, 0, 0), memory_space=pltpu.SMEM),
                pl.BlockSpec((1, 1, 2 * tm), lambda t: (jnp.minimum(t + 1, nt - 1), 0, 0),
                             memory_space=pltpu.SMEM),
                pl.BlockSpec((tm, d), row),
                pl.BlockSpec((tm, LANES), row),
                pl.BlockSpec(memory_space=pl.ANY)]
    args = [pos_t, pos_t, h, wts, ys]
    if n_norm:
        in_specs.append(pl.BlockSpec((n_norm, d), lambda t: (0, 0)))
        args.append(norm_gains.astype(F32))
    outs = pl.pallas_call(
        functools.partial(_combine_body, tm=tm, n_norm=n_norm),
        grid=(nt,),
        in_specs=in_specs,
        out_specs=[pl.BlockSpec((tm, d), row)] * (1 + n_norm),
        out_shape=[jax.ShapeDtypeStruct((n, d), F32)] + [jax.ShapeDtypeStruct((n, d), BF16)] * n_norm,
        scratch_shapes=[pltpu.VMEM((2, 2, tm * _slab_rows(d), LANES), ys.dtype),
                        pltpu.SemaphoreType.DMA((2,))],
        compiler_params=_cparams(("arbitrary",)),
        name="moe_combine",
    )(*args)
    return outs[0] if not n_norm else outs


def _routing_tables(ids, ranks, counts, tm):
    n = ids.shape[0]
    n_experts = counts.shape[0]
    padded = ((counts + tm - 1) // tm) * tm
    ends = jnp.cumsum(padded)
    offs = ends - padded
    experts = jnp.arange(n_experts, dtype=jnp.int32)
    pos = ranks + jnp.sum(jnp.where(ids[:, :, None] == experts[None, None, :], offs[None, None, :], 0), axis=-1)
    p_rows = ((2 * n + n_experts * (tm - 1)) // tm) * tm
    tile_start = jnp.arange(p_rows // tm, dtype=jnp.int32) * tm
    tile_expert = jnp.minimum(jnp.sum((ends[None, :] <= tile_start[:, None]).astype(jnp.int32), axis=1),
                              n_experts - 1)
    n_used = (ends[-1:] // tm).astype(jnp.int32)
    return pos.astype(jnp.int32), tile_expert, n_used, (offs + counts).astype(jnp.int32), \
        (padded - counts).astype(jnp.int32), p_rows


def hier_moe_block(h, gain, w_group, b_group, w_expert, b_expert, w_gate_up, w_down, layer, *,
                   tm=MOE_TILE, fc=512, norm_gains=None):
    xn, ids, ranks, wts, counts = router(h, gain, w_group, b_group, w_expert, b_expert)
    pos, tile_expert, n_used, pad_start, pad_count, p_rows = _routing_tables(ids, ranks, counts, tm)
    xs = moe_dispatch(xn, pos, pad_start, pad_count, n_used, p_rows, tm, _slab_rows(h.shape[1]))
    hs = moe_up(xs, tile_expert, n_used, w_gate_up, layer, tm, fc)
    ys = moe_down(hs, tile_expert, n_used, w_down, layer, tm)
    return moe_combine(h, wts, ys, pos, norm_gains)


def kernel(x, a_norm_g, a_w_in, a_w_decay_up, a_b_decay, a_out_norm_g, a_w_out, kv_norm_g, w_kv, k_norm_g,
           b_norm_g, b_w_q, b_q_norm_g, b_sinks, b_w_out, ffn_norm_g, moe_w_group, moe_b_group, moe_w_expert,
           moe_b_expert, moe_w_gate_up, moe_w_down):
    bsz, t, d = x.shape
    n = bsz * t
    h = x.reshape(n, d)
    tm_mm = 1024

    def moe(h, layer, norm_gains=None):
        return hier_moe_block(h, ffn_norm_g[layer], moe_w_group[layer], moe_b_group[layer],
                              moe_w_expert[layer], moe_b_expert[layer], moe_w_gate_up, moe_w_down, layer,
                              norm_gains=norm_gains)

    heads = GLA_HEADS
    dk_all = a_w_decay_up.shape[2]
    dk = dk_all // heads
    dv_all = a_w_out.shape[1]
    dv = dv_all // heads
    n_main = 2 * dk_all + 2 * dv_all
    w_in = a_w_in[0].astype(BF16)
    w_z = jnp.zeros((d, LANES), BF16).at[:, :GLA_GATE_RANK].set(a_w_in[0][:, n_main:].astype(BF16))
    xn, z = rmsnorm(h, a_norm_g[0:1], proj_w=w_z)
    proj = matmul(xn, w_in, n_cols=n_main, tm=tm_mm, tn=1024, out_dtype=BF16)
    gated = gla_core(proj, z, a_w_decay_up[0], a_b_decay[0], a_out_norm_g[0],
                     batch=bsz, heads=heads, dk=dk, dv=dv)
    h = matmul(gated, a_w_out[0].astype(BF16), n_cols=d, tm=tm_mm, tn=1024, res=h)
    h, xn_kv, xn_b = moe(h, 0, jnp.stack([kv_norm_g, b_norm_g[0]], axis=0))

    hd = SWA_HEAD_DIM
    kvh = SWA_KV_HEADS
    hq = b_w_q.shape[2] // hd
    w_kv_b = w_kv.astype(BF16)
    kw = kvh * hd
    kv_sh = matmul(xn_kv, w_kv_b, n_cols=2 * kw, tm=tm_mm, tn=2 * kw, out_dtype=BF16, head_dim=hd,
                   head_gain=k_norm_g, head_major=hd, norm_cols=kw)
    q = matmul(xn_b, b_w_q[0].astype(BF16), n_cols=hq * hd, tm=tm_mm, tn=1024, out_dtype=BF16)
    o = swa_attention(q, kv_sh, None, b_sinks[0], b_q_norm_g[0] * hd ** -0.5, batch=bsz)
    h = matmul(o, b_w_out[0].astype(BF16), n_cols=d, tm=tm_mm, tn=1024, res=h)
    h = moe(h, 1)
    return h.reshape(bsz, t, d)
```
